```python
import math
import jax, jax.numpy as jnp
from jax import lax
import numpy as np

D_MODEL = 4096
BATCH = 2
SEQ = 8192
DEPTH = 2

MIX_W = D_MODEL
HEAD_DIM = 128
ATTN_W = MIX_W // 2
N_HEAD_A = ATTN_W // HEAD_DIM
HEADS_PER_GROUP = 4
N_KV_GROUP = N_HEAD_A // HEADS_PER_GROUP
KV_W = N_KV_GROUP * HEAD_DIM
LRU_W = MIX_W // 4
LRU_BLOCKS = 8
LRU_BW = LRU_W // LRU_BLOCKS
LRU_CONV = 4
LRU_C = 8.0
GMLP_W = MIX_W // 4
GMLP_GROUPS = 8
GMLP_GW = GMLP_W // GMLP_GROUPS
CHUNK = 128
CMP_LEN = 32
CMP_STRIDE = 16
SEL_BLOCK = 64
SEL_TOPK = 16
WINDOW = 512
QBLK = 64
D_FF = 256 * (-(-8 * D_MODEL // (3 * 256)))
FFN_CONV = 3
MOD_INIT = 0.5
N_IN = ATTN_W + 6 * KV_W + 3 * N_HEAD_A + 2 * LRU_W + 2 * GMLP_W
NEG = -1e30
FORCE = 1e4

kernel_name = "hybrid_nsa_rglru_gmlp_deepnorm_block"


def layer_norm(x, g, b, eps=1e-5):
    xf = x.astype(jnp.float32)
    mu = jnp.mean(xf, -1, keepdims=True)
    var = jnp.mean(jnp.square(xf - mu), -1, keepdims=True)
    return ((xf - mu) * lax.rsqrt(var + eps) * g + b).astype(x.dtype)


def causal_dwconv(x, w, b):
    K = w.shape[0]
    S = x.shape[1]
    xp = jnp.pad(x, ((0, 0), (K - 1, 0), (0, 0)))
    y = xp[:, 0:S] * w[0]
    for k in range(1, K):
        y = y + xp[:, k:k + S] * w[k]
    return y + b


def split_cols(z, sizes):
    offs = np.cumsum(np.array(sizes))[:-1].tolist()
    return jnp.split(z, offs, axis=-1)


def nsa_compress(kv, pos, w1, b1, w2):
    B, S, G, DH = kv.shape
    r = CMP_LEN // CMP_STRIDE
    n_chunk = S // CMP_STRIDE
    nc = n_chunk - r + 1
    ch = kv.reshape(B, n_chunk, CMP_STRIDE, G, DH)
    blk = jnp.concatenate([ch[:, j:j + nc] for j in range(r)], axis=2)
    blk = blk + pos[:, None, :]
    flat = jnp.swapaxes(blk, 2, 3).reshape(B, nc, G, CMP_LEN * DH)
    return jax.nn.gelu(flat @ w1 + b1) @ w2


def nsa_attention(q, kc, vc, ks, vs, kw, vw, gates):
    B, S, G, HPG, DH = q.shape
    NC = kc.shape[1]
    nsel = S // SEL_BLOCK
    topk = min(SEL_TOPK, nsel)
    nqb = S // QBLK
    scale = DH ** -0.5
    c_start = jnp.arange(NC) * CMP_STRIDE
    c_end = c_start + CMP_LEN - 1
    s_start = jnp.arange(nsel) * SEL_BLOCK
    cover = ((c_start[:, None] < s_start[None, :] + SEL_BLOCK)
             & (c_start[:, None] + CMP_LEN > s_start[None, :])).astype(jnp.float32)
    ks_b = ks.reshape(B, nsel, SEL_BLOCK, G, DH).transpose(0, 3, 1, 2, 4)
    vs_b = vs.reshape(B, nsel, SEL_BLOCK, G, DH).transpose(0, 3, 1, 2, 4)
    kw_p = jnp.pad(kw, ((0, 0), (WINDOW, 0), (0, 0), (0, 0)))
    vw_p = jnp.pad(vw, ((0, 0), (WINDOW, 0), (0, 0), (0, 0)))
    q_b = q.reshape(B, nqb, QBLK, G, HPG, DH).transpose(1, 0, 2, 3, 4, 5)
    g_b = gates.reshape(B, nqb, QBLK, G, HPG, 3).transpose(1, 0, 2, 3, 4, 5)
    bi = jnp.arange(B)[:, None, None, None]
    gi = jnp.arange(G)[None, :, None, None]
    blk_ids = jnp.arange(nsel)

    def block(args):
        i, qi, gt = args
        t = i * QBLK + jnp.arange(QBLK)
        s = jnp.einsum('bqghd,bcgd->bghqc', qi, kc, preferred_element_type=jnp.float32) * scale
        valid = c_end[None, :] <= t[:, None]
        s = jnp.where(valid, s, NEG)
        p_c = jax.nn.softmax(s, axis=-1) * jnp.any(valid, -1)[:, None].astype(jnp.float32)
        o_c = jnp.einsum('bghqc,bcgd->bqghd', p_c.astype(vc.dtype), vc)
        imp = jnp.einsum('bghqc,cj->bgqj', p_c, cover)
        cur = (t // SEL_BLOCK)[:, None]
        forced = (blk_ids[None, :] == 0) | (blk_ids[None, :] == cur) | (blk_ids[None, :] == cur - 1)
        causal = s_start[None, :] <= t[:, None]
        imp = jnp.where(forced, FORCE, jnp.where(causal, imp, NEG))
        _, idx = lax.top_k(imp, topk)
        k_sel = ks_b[bi, gi, idx]
        v_sel = vs_b[bi, gi, idx]
        kpos = idx[..., None] * SEL_BLOCK + jnp.arange(SEL_BLOCK)
        ok = kpos <= t[None, None, :, None, None]
        s = jnp.einsum('bqghd,bgqkld->bghqkl', qi, k_sel, preferred_element_type=jnp.float32) * scale
        s = jnp.where(ok[:, :, None], s, NEG)
        p_s = jax.nn.softmax(s.reshape(B, G, HPG, QBLK, topk * SEL_BLOCK), axis=-1).reshape(s.shape)
        o_s = jnp.einsum('bghqkl,bgqkld->bqghd', p_s.astype(v_sel.dtype), v_sel)
        kwin = lax.dynamic_slice_in_dim(kw_p, i * QBLK, WINDOW + QBLK, axis=1)
        vwin = lax.dynamic_slice_in_dim(vw_p, i * QBLK, WINDOW + QBLK, axis=1)
        wpos = i * QBLK - WINDOW + jnp.arange(WINDOW + QBLK)
        okw = (wpos[None, :] >= 0) & (wpos[None, :] <= t[:, None]) & (t[:, None] - wpos[None, :] < WINDOW)
        s = jnp.einsum('bqghd,bkgd->bghqk', qi, kwin, preferred_element_type=jnp.float32) * scale
        s = jnp.where(okw, s, NEG)
        p_w = jax.nn.softmax(s, axis=-1)
        o_w = jnp.einsum('bghqk,bkgd->bqghd', p_w.astype(vwin.dtype), vwin)
        return gt[..., 0:1] * o_c + gt[..., 1:2] * o_s + gt[..., 2:3] * o_w

    out = lax.map(block, (jnp.arange(nqb), q_b, g_b))
    return out.transpose(1, 0, 2, 3, 4, 5).reshape(B, S, G * HPG * DH)


def rg_lru(x, w_a, b_a, w_x, b_x, lam):
    B, S, W = x.shape
    xb = x.reshape(B, S, LRU_BLOCKS, LRU_BW)
    r = jax.nn.sigmoid(jnp.einsum('bsnc,ncd->bsnd', xb, w_a).reshape(B, S, W) + b_a).astype(jnp.float32)
    ig = jax.nn.sigmoid(jnp.einsum('bsnc,ncd->bsnd', xb, w_x).reshape(B, S, W) + b_x)
    log_a = -LRU_C * r * jax.nn.softplus(-lam.astype(jnp.float32))
    a = jnp.exp(log_a)
    bt = jnp.sqrt(-jnp.expm1(2.0 * log_a)) * (ig * x).astype(jnp.float32)

    def combine(left, right):
        a1, b1 = left
        a2, b2 = right
        return a1 * a2, a2 * b1 + b2

    _, h = lax.associative_scan(combine, (a, bt), axis=1)
    return h.astype(x.dtype)


def spatial_gating(u, v, ln_g, ln_b, w_s, b_s):
    B, S, W = u.shape
    v = layer_norm(v, ln_g, ln_b)
    vb = v.reshape(B, S // CHUNK, CHUNK, GMLP_GROUPS, GMLP_GW)
    ws = jnp.where(jnp.tril(jnp.ones((CHUNK, CHUNK), dtype=bool)), w_s, 0.0)
    y = jnp.einsum('gts,bnsgc->bntgc', ws, vb) + b_s.T[None, None, :, :, None]
    return u * y.reshape(B, S, W)


def token_mixer(h, w_in, cmp_pos, cmp_w1, cmp_b1, cmp_w2, lru_conv_w, lru_conv_b,
                lru_wa, lru_ba, lru_wx, lru_bx, lru_lambda, sgu_ln_g, sgu_ln_b, sgu_w, sgu_b, w_o):
    B, S, _ = h.shape
    z = h @ w_in
    (q, kc, vc, ks, vs, kw, vw, gl, zg, zr, zu, zv) = split_cols(
        z, [ATTN_W] + [KV_W] * 6 + [3 * N_HEAD_A, LRU_W, LRU_W, GMLP_W, GMLP_W])
    kv_shape = (B, S, N_KV_GROUP, HEAD_DIM)
    q = q.reshape(B, S, N_KV_GROUP, HEADS_PER_GROUP, HEAD_DIM)
    kc = nsa_compress(kc.reshape(kv_shape), cmp_pos[0], cmp_w1[0], cmp_b1[0], cmp_w2[0])
    vc = nsa_compress(vc.reshape(kv_shape), cmp_pos[1], cmp_w1[1], cmp_b1[1], cmp_w2[1])
    gates = jax.nn.sigmoid(gl).reshape(B, S, N_KV_GROUP, HEADS_PER_GROUP, 3)
    y_attn = nsa_attention(q, kc, vc, ks.reshape(kv_shape), vs.reshape(kv_shape),
                           kw.reshape(kv_shape), vw.reshape(kv_shape), gates)
    xr = causal_dwconv(zr, lru_conv_w, lru_conv_b)
    y_lru = jax.nn.gelu(zg) * rg_lru(xr, lru_wa, lru_ba, lru_wx, lru_bx, lru_lambda)
    y_sgu = spatial_gating(jax.nn.gelu(zu), jax.nn.gelu(zv), sgu_ln_g, sgu_ln_b, sgu_w, sgu_b)
    return jnp.concatenate([y_attn, y_lru, y_sgu], axis=-1) @ w_o


def conv_ffn(h, w_in, conv_w, conv_b, w_down):
    g, u = jnp.split(h @ w_in, 2, axis=-1)
    g = causal_dwconv(g, conv_w, conv_b)
    return (jax.nn.silu(g) * u) @ w_down


def setup_inputs(seed: int = 0) -> dict:
    key = jax.random.key(seed)
    ks = jax.random.split(key, 32)
    f32 = jnp.float32
    L, D = DEPTH, D_MODEL
    beta = (8 * DEPTH) ** -0.25

    def nrm(k, shape, s):
        return jax.random.normal(k, shape, f32) * s

    u = jax.random.uniform(ks[15], (L, LRU_W), f32, 0.9, 0.999)
    p = u ** (1.0 / LRU_C)
    return {
        "x": nrm(ks[0], (BATCH, SEQ, D), 1.0),
        "c": nrm(ks[1], (BATCH, D), 1.0),
        "w_mod": nrm(ks[2], (L, D, 6 * D), MOD_INIT * D ** -0.5),
        "b_mod": nrm(ks[3], (L, 6 * D), 0.02),
        "w_in": nrm(ks[4], (L, D, N_IN), D ** -0.5),
        "cmp_pos": nrm(ks[5], (L, 2, CMP_LEN, HEAD_DIM), 0.1),
        "cmp_w1": nrm(ks[6], (L, 2, CMP_LEN * HEAD_DIM, HEAD_DIM), (CMP_LEN * HEAD_DIM) ** -0.5),
        "cmp_b1": nrm(ks[7], (L, 2, HEAD_DIM), 0.02),
        "cmp_w2": nrm(ks[8], (L, 2, HEAD_DIM, HEAD_DIM), HEAD_DIM ** -0.5),
        "lru_conv_w": nrm(ks[9], (L, LRU_CONV, LRU_W), LRU_CONV ** -0.5),
        "lru_conv_b": nrm(ks[10], (L, LRU_W), 0.02),
        "lru_wa": nrm(ks[11], (L, LRU_BLOCKS, LRU_BW, LRU_BW), LRU_BW ** -0.5),
        "lru_ba": nrm(ks[12], (L, LRU_W), 0.02),
        "lru_wx": nrm(ks[13], (L, LRU_BLOCKS, LRU_BW, LRU_BW), LRU_BW ** -0.5),
        "lru_bx": nrm(ks[14], (L, LRU_W), 0.02),
        "lru_lambda": jnp.log(p) - jnp.log1p(-p),
        "sgu_ln_g": 1.0 + nrm(ks[16], (L, GMLP_W), 0.02),
        "sgu_ln_b": nrm(ks[17], (L, GMLP_W), 0.02),
        "sgu_w": nrm(ks[18], (L, GMLP_GROUPS, CHUNK, CHUNK), CHUNK ** -0.5),
        "sgu_b": 1.0 + nrm(ks[19], (L, GMLP_GROUPS, CHUNK), 0.1),
        "w_o": nrm(ks[20], (L, MIX_W, D), beta * MIX_W ** -0.5),
        "ln1_g": 1.0 + nrm(ks[21], (L, D), 0.02),
        "ln1_b": nrm(ks[22], (L, D), 0.02),
        "ffn_w_in": nrm(ks[23], (L, D, 2 * D_FF), D ** -0.5),
        "ffn_conv_w": nrm(ks[24], (L, FFN_CONV, D_FF), FFN_CONV ** -0.5),
        "ffn_conv_b": nrm(ks[25], (L, D_FF), 0.02),
        "ffn_w_down": nrm(ks[26], (L, D_FF, D), beta * D_FF ** -0.5),
        "ln2_g": 1.0 + nrm(ks[27], (L, D), 0.02),
        "ln2_b": nrm(ks[28], (L, D), 0.02),
    }


def reference(x, c, w_mod, b_mod, w_in, cmp_pos, cmp_w1, cmp_b1, cmp_w2, lru_conv_w, lru_conv_b,
              lru_wa, lru_ba, lru_wx, lru_bx, lru_lambda, sgu_ln_g, sgu_ln_b, sgu_w, sgu_b, w_o,
              ln1_g, ln1_b, ffn_w_in, ffn_conv_w, ffn_conv_b, ffn_w_down, ln2_g, ln2_b):
    alpha = (2.0 * DEPTH) ** 0.25
    cs = jax.nn.silu(c)
    for l in range(DEPTH):
        mod = (cs @ w_mod[l] + b_mod[l])[:, None, :]
        sh_m, sc_m, g_m, sh_f, sc_f, g_f = jnp.split(mod, 6, axis=-1)
        h = x * (1.0 + sc_m) + sh_m
        y = token_mixer(h, w_in[l], cmp_pos[l], cmp_w1[l], cmp_b1[l], cmp_w2[l], lru_conv_w[l],
                        lru_conv_b[l], lru_wa[l], lru_ba[l], lru_wx[l], lru_bx[l], lru_lambda[l],
                        sgu_ln_g[l], sgu_ln_b[l], sgu_w[l], sgu_b[l], w_o[l])
        x = layer_norm(alpha * x + (1.0 + g_m) * y, ln1_g[l], ln1_b[l])
        h = x * (1.0 + sc_f) + sh_f
        y = conv_ffn(h, ffn_w_in[l], ffn_conv_w[l], ffn_conv_b[l], ffn_w_down[l])
        x = layer_norm(alpha * x + (1.0 + g_f) * y, ln2_g[l], ln2_b[l])
    return x
```

```python
import functools

import jax
import jax.numpy as jnp
import numpy as np
from jax import lax
from jax.experimental import pallas as pl
from jax.experimental.pallas import tpu as pltpu

F32 = jnp.float32
BF16 = jnp.bfloat16

DEPTH = 2
HEAD_DIM = 128
HPG = 4
N_GROUP = 4
ATTN_W = HPG * N_GROUP * HEAD_DIM
KV_W = N_GROUP * HEAD_DIM
LRU_W = 1024
LRU_BLOCKS = 8
LRU_CONV = 4
LRU_C = 8.0
GMLP_W = 1024
GMLP_GROUPS = 8
CHUNK = 128
CMP_LEN = 32
CMP_STRIDE = 16
SEL_BLOCK = 64
SEL_TOPK = 16
WINDOW = 512
FFN_CONV = 3
NEG = -1e30
FORCE = 1e4
LN_EPS = 1e-5
ALPHA = (2.0 * DEPTH) ** 0.25
SCALE = HEAD_DIM ** -0.5
N_GATE = 3 * HPG
SEL_PAD = 128

V7X_VMEM_LIMIT_BYTES = 58 * 1024 * 1024
LANE = 128
SUBLANE = 8

ZB_KC, ZB_VC, ZB_G, ZB_R, ZB_U, ZB_V, ZB_GL = 0, 512, 1024, 2048, 3072, 4096, 5120
ZB_COLS = 5632


def _params(sem):
    return pltpu.CompilerParams(dimension_semantics=sem, vmem_limit_bytes=V7X_VMEM_LIMIT_BYTES)


def _gelu(x):
    return 0.5 * x * (1.0 + jnp.tanh(0.7978845608028654 * (x + 0.044715 * (x * x * x))))


def _layer_norm(v, g, b):
    mu = jnp.mean(v, axis=-1, keepdims=True)
    d = v - mu
    var = jnp.mean(d * d, axis=-1, keepdims=True)
    return d * lax.rsqrt(var + LN_EPS) * g + b


def _shift_rows(cur, prev_tail, k):
    r = pltpu.roll(cur, k, 0)
    rp = pltpu.roll(prev_tail, k, 0)
    row = lax.broadcasted_iota(jnp.int32, prev_tail.shape, 0)
    first = jnp.where(row < k, rp, r[0:SUBLANE])
    return jnp.concatenate([first, r[SUBLANE:]], axis=0)


def _modulate_into(h_ref, x_ref, sc_ref, sh_ref, row_chunk=128):
    scale = 1.0 + sc_ref[...]
    shift = sh_ref[...]

    def body(r, _):
        rs = pl.ds(pl.multiple_of(r * row_chunk, row_chunk), row_chunk)
        h_ref[rs, :] = (x_ref[rs, :] * scale + shift).astype(h_ref.dtype)
        return 0

    lax.fori_loop(0, h_ref.shape[0] // row_chunk, body, 0)


def _accumulate_matmul(o_ref, lhs, w_ref, step, col_chunk=1024):
    n = o_ref.shape[1]
    for c0 in range(0, n, col_chunk):
        cs = slice(c0, min(c0 + col_chunk, n))
        part = jnp.dot(lhs, w_ref[:, cs], preferred_element_type=F32)

        @pl.when(step == 0)
        def _():
            o_ref[:, cs] = part

        @pl.when(step > 0)
        def _():
            o_ref[:, cs] += part


def _deepnorm_epilogue(o_ref, x_ref, gm_ref, lng_ref, lnb_ref, row_chunk=128):
    gate = 1.0 + gm_ref[...]
    lng = lng_ref[...]
    lnb = lnb_ref[...]

    def body(r, _):
        rs = pl.ds(pl.multiple_of(r * row_chunk, row_chunk), row_chunk)
        v = ALPHA * x_ref[rs, :] + gate * o_ref[rs, :]
        o_ref[rs, :] = _layer_norm(v, lng, lnb)
        return 0

    lax.fori_loop(0, o_ref.shape[0] // row_chunk, body, 0)


def _mod_kernel(c_ref, w_ref, b_ref, o_ref):
    c = c_ref[...]
    cs = (c * jax.nn.sigmoid(c)).astype(BF16)
    o_ref[...] = jnp.dot(cs, w_ref[...].astype(BF16), preferred_element_type=F32) + b_ref[...]


def _modulation(c_pad, w_mod, b_mod, tn=512):
    depth, d, n = w_mod.shape
    rows = c_pad.shape[0]
    return pl.pallas_call(
        _mod_kernel,
        grid=(depth, n // tn),
        in_specs=[
            pl.BlockSpec((rows, d), lambda l, j: (0, 0)),
            pl.BlockSpec((None, d, tn), lambda l, j: (l, 0, j)),
            pl.BlockSpec((None, 1, tn), lambda l, j: (l, 0, j)),
        ],
        out_specs=pl.BlockSpec((None, rows, tn), lambda l, j: (l, 0, j)),
        out_shape=jax.ShapeDtypeStruct((depth, rows, n), F32),
        compiler_params=_params(("arbitrary", "arbitrary")),
        name="modulation",
    )(c_pad, w_mod, b_mod.reshape(depth, 1, n))


def _modmm_kernel(x_ref, sc_ref, sh_ref, w_ref, o_ref, h_ref):
    @pl.when(pl.program_id(1) == 0)
    def _():
        _modulate_into(h_ref, x_ref, sc_ref, sh_ref)

    o_ref[...] = jnp.dot(h_ref[...], w_ref[...], preferred_element_type=F32).astype(o_ref.dtype)


def _modmm(x, mod_l, sc_idx, sh_idx, w, out_dtype, rows_per_batch, tm=512, tn=512, name="modmm"):
    m, k = x.shape
    n = w.shape[1]
    tpb = rows_per_batch // tm
    return pl.pallas_call(
        _modmm_kernel,
        grid=(m // tm, n // tn),
        in_specs=[
            pl.BlockSpec((tm, k), lambda i, j: (i, 0)),
            pl.BlockSpec((None, 1, k), lambda i, j: (i // tpb, 0, sc_idx)),
            pl.BlockSpec((None, 1, k), lambda i, j: (i // tpb, 0, sh_idx)),
            pl.BlockSpec((k, tn), lambda i, j: (0, j)),
        ],
        out_specs=pl.BlockSpec((tm, tn), lambda i, j: (i, j)),
        out_shape=jax.ShapeDtypeStruct((m, n), out_dtype),
        scratch_shapes=[pltpu.VMEM((tm, k), BF16)],
        compiler_params=_params(("arbitrary", "arbitrary")),
        name=name,
    )(x, mod_l, mod_l, w)


def _compress_kernel(kv_ref, pos_ref, w1_ref, b1_ref, w2_ref, o_ref, *, ncp):
    a0 = jnp.zeros((ncp, HEAD_DIM), F32)
    a1 = jnp.zeros((ncp, HEAD_DIM), F32)
    for r in range(CMP_STRIDE):
        xr = kv_ref[pl.ds(r, ncp, stride=CMP_STRIDE), :]
        lo = (xr + pos_ref[r:r + 1, :]).astype(BF16)
        hi = (xr + pos_ref[CMP_STRIDE + r:CMP_STRIDE + r + 1, :]).astype(BF16)
        a0 = a0 + jnp.dot(lo, w1_ref[r * HEAD_DIM:(r + 1) * HEAD_DIM, :], preferred_element_type=F32)
        a1 = a1 + jnp.dot(hi, w1_ref[(CMP_STRIDE + r) * HEAD_DIM:(CMP_STRIDE + r + 1) * HEAD_DIM, :],
                          preferred_element_type=F32)
    pre = a0 + pltpu.roll(a1, ncp - 1, 0) + b1_ref[...]
    hid = _gelu(pre).astype(BF16)
    o_ref[...] = jnp.dot(hid, w2_ref[...], preferred_element_type=F32).astype(o_ref.dtype)


def _compress(zb, cmp_pos, cmp_w1, cmp_b1, cmp_w2, batch, seq):
    ncp = seq // CMP_STRIDE
    kern = functools.partial(_compress_kernel, ncp=ncp)
    return pl.pallas_call(
        kern,
        grid=(2, batch, N_GROUP),
        in_specs=[
            pl.BlockSpec((seq, HEAD_DIM), lambda w, b, g: (b, w * N_GROUP + g)),
            pl.BlockSpec((None, CMP_LEN, HEAD_DIM), lambda w, b, g: (w, 0, 0)),
            pl.BlockSpec((None, CMP_LEN * HEAD_DIM, HEAD_DIM), lambda w, b, g: (w, 0, 0)),
            pl.BlockSpec((None, 1, HEAD_DIM), lambda w, b, g: (w, 0, 0)),
            pl.BlockSpec((None, HEAD_DIM, HEAD_DIM), lambda w, b, g: (w, 0, 0)),
        ],
        out_specs=pl.BlockSpec((None, None, None, ncp, HEAD_DIM), lambda w, b, g: (w, b, g, 0, 0)),
        out_shape=jax.ShapeDtypeStruct((2, batch, N_GROUP, ncp, HEAD_DIM), BF16),
        compiler_params=_params(("arbitrary", "arbitrary", "arbitrary")),
        name="nsa_compress",
    )(zb, cmp_pos, cmp_w1.astype(BF16), cmp_b1.reshape(2, 1, HEAD_DIM), cmp_w2.astype(BF16))


_NT = (((1,), (1,)), ((), ()))


def _attn_kernel(q_ref, ks_ref, vs_ref, kw_ref, vw_ref, ke_ref, kc_ref, vct_ref, cov_ref, gl_ref,
                 o_ref, qa_ref, *, tq, tk, ncp, nsel):
    rows = HPG * tq
    q0 = pl.program_id(2) * tq
    for h in range(HPG):
        qa_ref[h * tq:(h + 1) * tq, 0:HEAD_DIM] = q_ref[:, h * HEAD_DIM:(h + 1) * HEAD_DIM]
    q4 = qa_ref[:, 0:HEAD_DIM]

    s_c = lax.dot_general(kc_ref[...], q4, _NT, preferred_element_type=F32) * SCALE
    ci = lax.broadcasted_iota(jnp.int32, (ncp, rows), 0)
    tt = q0 + (lax.broadcasted_iota(jnp.int32, (ncp, rows), 1) & (tq - 1))
    s_c = jnp.where(ci * CMP_STRIDE + (CMP_LEN - 1) <= tt, s_c, NEG)
    m_c = jnp.max(s_c, axis=0, keepdims=True)
    p_c = jnp.exp(s_c - m_c)
    l_c = jnp.sum(p_c, axis=0, keepdims=True)
    t_row = q0 + (lax.broadcasted_iota(jnp.int32, (1, rows), 1) & (tq - 1))
    p_c = p_c * ((t_row >= CMP_LEN - 1).astype(F32) / l_c)
    o_ct = jnp.dot(vct_ref[...], p_c.astype(BF16), preferred_element_type=F32)

    p_sum = p_c[:, 0:tq]
    for h in range(1, HPG):
        p_sum = p_sum + p_c[:, h * tq:(h + 1) * tq]
    imp = jnp.dot(cov_ref[...], p_sum.astype(BF16), preferred_element_type=F32)
    jb = lax.broadcasted_iota(jnp.int32, (SEL_PAD, tq), 0)
    tl = q0 + lax.broadcasted_iota(jnp.int32, (SEL_PAD, tq), 1)
    cur = lax.shift_right_logical(tl, 6)
    forced = (jb == 0) | (jb == cur) | (jb == cur - 1)
    imp = jnp.where(forced, FORCE, jnp.where(jb * SEL_BLOCK <= tl, imp, NEG))
    if nsel < SEL_PAD:
        imp = jnp.where(jb < nsel, imp, -jnp.inf)
    selb = jnp.full((SEL_PAD, tq), NEG, F32)
    for _ in range(min(SEL_TOPK, nsel)):
        mx = jnp.max(imp, axis=0, keepdims=True)
        idx = jnp.min(jnp.where(imp == mx, jb, SEL_PAD), axis=0, keepdims=True)
        hit = jb == idx
        selb = jnp.where(hit, 0.0, selb)
        imp = jnp.where(hit, -jnp.inf, imp)
    selb_t = selb.T.astype(BF16)
    for h in range(HPG):
        qa_ref[h * tq:(h + 1) * tq, HEAD_DIM:HEAD_DIM + SEL_PAD] = selb_t

    a = q0 // tk
    row_t = q0 + (lax.broadcasted_iota(jnp.int32, (rows, tk), 0) & (tq - 1))
    col = lax.broadcasted_iota(jnp.int32, (rows, tk), 1)

    def online(carry, s, v):
        m, l, acc = carry
        mn = jnp.maximum(m, jnp.max(s, axis=1, keepdims=True))
        al = jnp.exp(m - mn)
        p = jnp.exp(s - mn)
        l = al * l + jnp.sum(p, axis=1, keepdims=True)
        acc = al * acc + jnp.dot(p.astype(BF16), v, preferred_element_type=F32)
        return mn, l, acc

    def sel_tile(kt, carry, masked):
        start = pl.multiple_of(kt * tk, tk)
        kaug = jnp.concatenate([ks_ref[pl.ds(start, tk), :], ke_ref[pl.ds(start, tk), :]], axis=1)
        s = lax.dot_general(qa_ref[...], kaug, _NT, preferred_element_type=F32) * SCALE
        if masked:
            s = jnp.where(kt * tk + col <= row_t, s, NEG)
        return online(carry, s, vs_ref[pl.ds(start, tk), :])

    init = (jnp.full((rows, 1), NEG, F32), jnp.zeros((rows, 1), F32), jnp.zeros((rows, HEAD_DIM), F32))
    carry = lax.fori_loop(0, a, lambda kt, c: sel_tile(kt, c, False), init)
    _, l_s, acc_s = sel_tile(a, carry, True)
    o_s = acc_s * (1.0 / l_s)

    def win_tile(kt, carry):
        start = pl.multiple_of(jnp.maximum(kt, 0) * tk, tk)
        s = lax.dot_general(q4, kw_ref[pl.ds(start, tk), :], _NT, preferred_element_type=F32) * SCALE
        kpos = kt * tk + col
        ok = (kpos >= 0) & (kpos <= row_t) & (row_t - kpos < WINDOW)
        s = jnp.where(ok, s, NEG)
        return online(carry, s, vw_ref[pl.ds(start, tk), :])

    carry = win_tile(a - 1, init)
    _, l_w, acc_w = win_tile(a, carry)
    o_w = acc_w * (1.0 / l_w)

    gate = jax.nn.sigmoid(gl_ref[...])
    for h in range(HPG):
        sl = slice(h * tq, (h + 1) * tq)
        o_c = o_ct[:, sl].T
        out = (gate[:, 3 * h:3 * h + 1] * o_c + gate[:, 3 * h + 1:3 * h + 2] * o_s[sl]
               + gate[:, 3 * h + 2:3 * h + 3] * o_w[sl])
        o_ref[:, h * HEAD_DIM:(h + 1) * HEAD_DIM] = out.astype(o_ref.dtype)


def _selection_constants(seq):
    ncp = seq // CMP_STRIDE
    nc = ncp - CMP_LEN // CMP_STRIDE + 1
    nsel = seq // SEL_BLOCK
    ci = np.arange(ncp)[None, :] * CMP_STRIDE
    sj = np.arange(SEL_PAD)[:, None] * SEL_BLOCK
    cover_t = ((ci < sj + SEL_BLOCK) & (ci + CMP_LEN > sj)
               & (np.arange(ncp)[None, :] < nc) & (np.arange(SEL_PAD)[:, None] < nsel))
    key_block = (np.arange(seq)[:, None] // SEL_BLOCK) == np.arange(SEL_PAD)[None, :]
    return jnp.asarray(cover_t, BF16), jnp.asarray(key_block, BF16)


def _attention(za, kc, vct, gl, batch, seq, tq=128, tk=512):
    assert tk == WINDOW and tk % tq == 0 and tq & (tq - 1) == 0 and seq % tk == 0
    ncp = seq // CMP_STRIDE
    nsel = seq // SEL_BLOCK
    nq = seq // tq
    cover_t, key_block = _selection_constants(seq)
    qcols = ATTN_W // LANE
    kern = functools.partial(_attn_kernel, tq=tq, tk=tk, ncp=ncp, nsel=nsel)
    kv_spec = lambda off: pl.BlockSpec((seq, HEAD_DIM), lambda b, g, i: (b, qcols + off * N_GROUP + g))
    return pl.pallas_call(
        kern,
        grid=(batch, N_GROUP, nq),
        in_specs=[
            pl.BlockSpec((tq, HPG * HEAD_DIM), lambda b, g, i: (b * nq + i, g)),
            kv_spec(0), kv_spec(1), kv_spec(2), kv_spec(3),
            pl.BlockSpec((seq, SEL_PAD), lambda b, g, i: (0, 0)),
            pl.BlockSpec((None, None, ncp, HEAD_DIM), lambda b, g, i: (b, g, 0, 0)),
            pl.BlockSpec((None, None, HEAD_DIM, ncp), lambda b, g, i: (b, g, 0, 0)),
            pl.BlockSpec((SEL_PAD, ncp), lambda b, g, i: (0, 0)),
            pl.BlockSpec((None, tq, N_GATE), lambda b, g, i: (g, b * nq + i, 0)),
        ],
        out_specs=pl.BlockSpec((tq, HPG * HEAD_DIM), lambda b, g, i: (b * nq + i, g)),
        out_shape=jax.ShapeDtypeStruct((batch * seq, ATTN_W), BF16),
        scratch_shapes=[pltpu.VMEM((HPG * tq, HEAD_DIM + SEL_PAD), BF16)],
        compiler_params=_params(("arbitrary", "arbitrary", "arbitrary")),
        name="nsa_attention",
    )(za, za, za, za, za, key_block, kc, vct, cover_t, gl)


def _lru_kernel(zg_ref, zr_ref, cw_ref, cb_ref, wa_ref, ba_ref, wx_ref, bx_ref, lam_ref, o_ref,
                tail_ref, h_ref, *, tt):
    @pl.when(pl.program_id(1) == 0)
    def _():
        tail_ref[...] = jnp.zeros_like(tail_ref)
        h_ref[...] = jnp.zeros_like(h_ref)

    zr = zr_ref[...]
    width = zr.shape[1]
    prev = tail_ref[...]
    xr = cw_ref[LRU_CONV - 1:LRU_CONV, :] * zr + cb_ref[...]
    for k in range(1, LRU_CONV):
        xr = xr + cw_ref[LRU_CONV - 1 - k:LRU_CONV - k, :] * _shift_rows(zr, prev, k)
    tail_ref[...] = zr[tt - SUBLANE:tt]

    xb = xr.astype(BF16)
    bw = width // LRU_BLOCKS
    r_parts, i_parts = [], []
    for n in range(LRU_BLOCKS):
        xs = xb[:, n * bw:(n + 1) * bw]
        r_parts.append(jnp.dot(xs, wa_ref[n], preferred_element_type=F32))
        i_parts.append(jnp.dot(xs, wx_ref[n], preferred_element_type=F32))
    rg = jax.nn.sigmoid(jnp.concatenate(r_parts, axis=1) + ba_ref[...])
    ig = jax.nn.sigmoid(jnp.concatenate(i_parts, axis=1) + bx_ref[...])
    lam = lam_ref[...]
    softplus_neg = jnp.maximum(-lam, 0.0) + jnp.log1p(jnp.exp(-jnp.abs(lam)))
    log_a = (-LRU_C) * rg * softplus_neg
    a = jnp.exp(log_a)
    b = jnp.sqrt(1.0 - a * a) * (ig * xr)

    row = lax.broadcasted_iota(jnp.int32, (tt, width), 0)
    d = 1
    while d < tt:
        a_s = pltpu.roll(a, d, 0)
        b_s = pltpu.roll(b, d, 0)
        keep = row >= d
        b = jnp.where(keep, a * b_s + b, b)
        a = jnp.where(keep, a * a_s, a)
        d *= 2
    h = a * h_ref[...] + b
    h_ref[...] = h[tt - 1:tt]
    o_ref[...] = (_gelu(zg_ref[...]) * h).astype(o_ref.dtype)


def _lru(zb, conv_w, conv_b, wa, ba, wx, bx, lam, batch, seq, tt=256):
    nt = seq // tt
    cg, cr = ZB_G // LRU_W, ZB_R // LRU_W
    kern = functools.partial(_lru_kernel, tt=tt)
    vec = lambda: pl.BlockSpec((1, LRU_W), lambda b, t: (0, 0))
    mat = lambda: pl.BlockSpec((LRU_BLOCKS, LRU_W // LRU_BLOCKS, LRU_W // LRU_BLOCKS), lambda b, t: (0, 0, 0))
    return pl.pallas_call(
        kern,
        grid=(batch, nt),
        in_specs=[
            pl.BlockSpec((tt, LRU_W), lambda b, t: (b * nt + t, cg)),
            pl.BlockSpec((tt, LRU_W), lambda b, t: (b * nt + t, cr)),
            pl.BlockSpec((LRU_CONV, LRU_W), lambda b, t: (0, 0)),
            vec(), mat(), vec(), mat(), vec(), vec(),
        ],
        out_specs=pl.BlockSpec((tt, LRU_W), lambda b, t: (b * nt + t, 0)),
        out_shape=jax.ShapeDtypeStruct((batch * seq, LRU_W), BF16),
        scratch_shapes=[pltpu.VMEM((SUBLANE, LRU_W), F32), pltpu.VMEM((1, LRU_W), F32)],
        compiler_params=_params(("arbitrary", "arbitrary")),
        name="rg_lru",
    )(zb, zb, conv_w, conv_b.reshape(1, LRU_W), wa.astype(BF16), ba.reshape(1, LRU_W),
      wx.astype(BF16), bx.reshape(1, LRU_W), lam.reshape(1, LRU_W))


def _sgu_kernel(zu_ref, zv_ref, g_ref, b_ref, ws_ref, bst_ref, o_ref, *, nchunk):
    v = _layer_norm(_gelu(zv_ref[...]), g_ref[...], b_ref[...]).astype(BF16)
    u = _gelu(zu_ref[...])
    gw = GMLP_W // GMLP_GROUPS
    tril = (lax.broadcasted_iota(jnp.int32, (CHUNK, CHUNK), 0)
            >= lax.broadcasted_iota(jnp.int32, (CHUNK, CHUNK), 1))
    for gi in range(GMLP_GROUPS):
        cs = slice(gi * gw, (gi + 1) * gw)
        w = jnp.where(tril, ws_ref[gi], 0.0).astype(BF16)
        rhs = jnp.concatenate([v[c * CHUNK:(c + 1) * CHUNK, cs] for c in range(nchunk)], axis=1)
        y = jnp.dot(w, rhs, preferred_element_type=F32) + bst_ref[:, gi:gi + 1]
        for c in range(nchunk):
            rs = slice(c * CHUNK, (c + 1) * CHUNK)
            o_ref[rs, cs] = (u[rs, cs] * y[:, c * gw:(c + 1) * gw]).astype(o_ref.dtype)


def _sgu(zb, ln_g, ln_b, w_s, b_s, batch, seq, nchunk=4):
    tt = nchunk * CHUNK
    nt = batch * seq // tt
    cu, cv = ZB_U // GMLP_W, ZB_V // GMLP_W
    kern = functools.partial(_sgu_kernel, nchunk=nchunk)
    return pl.pallas_call(
        kern,
        grid=(nt,),
        in_specs=[
            pl.BlockSpec((tt, GMLP_W), lambda t: (t, cu)),
            pl.BlockSpec((tt, GMLP_W), lambda t: (t, cv)),
            pl.BlockSpec((1, GMLP_W), lambda t: (0, 0)),
            pl.BlockSpec((1, GMLP_W), lambda t: (0, 0)),
            pl.BlockSpec((GMLP_GROUPS, CHUNK, CHUNK), lambda t: (0, 0, 0)),
            pl.BlockSpec((CHUNK, GMLP_GROUPS), lambda t: (0, 0)),
        ],
        out_specs=pl.BlockSpec((tt, GMLP_W), lambda t: (t, 0)),
        out_shape=jax.ShapeDtypeStruct((batch * seq, GMLP_W), BF16),
        compiler_params=_params(("arbitrary",)),
        name="spatial_gating",
    )(zb, zb, ln_g.reshape(1, GMLP_W), ln_b.reshape(1, GMLP_W), w_s, b_s.T)


def _oproj_kernel(y_ref, w_ref, x_ref, gm_ref, lng_ref, lnb_ref, o_ref):
    k = pl.program_id(1)
    _accumulate_matmul(o_ref, y_ref[...], w_ref, k)

    @pl.when(k == pl.num_programs(1) - 1)
    def _():
        _deepnorm_epilogue(o_ref, x_ref, gm_ref, lng_ref, lnb_ref)


def _oproj(y, w, x, mod_l, gate_idx, ln_g, ln_b, rows_per_batch, tm=512, tk=512):
    m, kdim = y.shape
    d = w.shape[1]
    tpb = rows_per_batch // tm
    return pl.pallas_call(
        _oproj_kernel,
        grid=(m // tm, kdim // tk),
        in_specs=[
            pl.BlockSpec((tm, tk), lambda i, k: (i, k)),
            pl.BlockSpec((tk, d), lambda i, k: (k, 0)),
            pl.BlockSpec((tm, d), lambda i, k: (i, 0), pipeline_mode=pl.Buffered(1)),
            pl.BlockSpec((None, 1, d), lambda i, k: (i // tpb, 0, gate_idx)),
            pl.BlockSpec((1, d), lambda i, k: (0, 0)),
            pl.BlockSpec((1, d), lambda i, k: (0, 0)),
        ],
        out_specs=pl.BlockSpec((tm, d), lambda i, k: (i, 0)),
        out_shape=jax.ShapeDtypeStruct((m, d), F32),
        compiler_params=_params(("arbitrary", "arbitrary")),
        name="out_proj_ln",
    )(y, w, x, mod_l, ln_g.reshape(1, d), ln_b.reshape(1, d))


def _ffn_kernel(x_ref, sc_ref, sh_ref, gm_ref, wg_ref, wu_ref, cw_ref, cb_ref, wd_ref, lng_ref, lnb_ref,
                o_ref, h_ref, tail_ref, *, tm, tpb):
    i = pl.program_id(0)
    f = pl.program_id(1)

    @pl.when(f == 0)
    def _():
        _modulate_into(h_ref, x_ref, sc_ref, sh_ref)

    h = h_ref[...]
    g = jnp.dot(h, wg_ref[...], preferred_element_type=F32)
    u = jnp.dot(h, wu_ref[...], preferred_element_type=F32)
    prev = jnp.where(i % tpb == 0, 0.0, tail_ref[f])
    tail_ref[f] = g[tm - SUBLANE:tm]
    gc = cw_ref[FFN_CONV - 1:FFN_CONV, :] * g + cb_ref[...]
    for k in range(1, FFN_CONV):
        gc = gc + cw_ref[FFN_CONV - 1 - k:FFN_CONV - k, :] * _shift_rows(g, prev, k)
    act = (gc * jax.nn.sigmoid(gc) * u).astype(BF16)
    _accumulate_matmul(o_ref, act, wd_ref, f)

    @pl.when(f == pl.num_programs(1) - 1)
    def _():
        _deepnorm_epilogue(o_ref, x_ref, gm_ref, lng_ref, lnb_ref)


def _ffn(x, mod_l, sc_idx, sh_idx, gate_idx, w_in, conv_w, conv_b, w_down, ln_g, ln_b, rows_per_batch,
         tm=512, tf=256):
    m, d = x.shape
    dff = w_down.shape[0]
    nf = dff // tf
    tpb = rows_per_batch // tm
    kern = functools.partial(_ffn_kernel, tm=tm, tpb=tpb)
    return pl.pallas_call(
        kern,
        grid=(m // tm, nf),
        in_specs=[
            pl.BlockSpec((tm, d), lambda i, f: (i, 0), pipeline_mode=pl.Buffered(1)),
            pl.BlockSpec((None, 1, d), lambda i, f: (i // tpb, 0, sc_idx)),
            pl.BlockSpec((None, 1, d), lambda i, f: (i // tpb, 0, sh_idx)),
            pl.BlockSpec((None, 1, d), lambda i, f: (i // tpb, 0, gate_idx)),
            pl.BlockSpec((d, tf), lambda i, f: (0, f)),
            pl.BlockSpec((d, tf), lambda i, f: (0, nf + f)),
            pl.BlockSpec((FFN_CONV, tf), lambda i, f: (0, f)),
            pl.BlockSpec((1, tf), lambda i, f: (0, f)),
            pl.BlockSpec((tf, d), lambda i, f: (f, 0)),
            pl.BlockSpec((1, d), lambda i, f: (0, 0)),
            pl.BlockSpec((1, d), lambda i, f: (0, 0)),
        ],
        out_specs=pl.BlockSpec((tm, d), lambda i, f: (i, 0)),
        out_shape=jax.ShapeDtypeStruct((m, d), F32),
        scratch_shapes=[pltpu.VMEM((tm, d), BF16), pltpu.VMEM((nf, SUBLANE, tf), F32)],
        compiler_params=_params(("arbitrary", "arbitrary")),
        name="conv_ffn_ln",
    )(x, mod_l, mod_l, mod_l, w_in, w_in, conv_w, conv_b.reshape(1, dff), w_down,
      ln_g.reshape(1, d), ln_b.reshape(1, d))


def _split_w_in(w):
    o = np.cumsum([0, ATTN_W] + [KV_W] * 6 + [N_GATE * N_GROUP, LRU_W, LRU_W, GMLP_W, GMLP_W]).tolist()
    q, kc, vc, ks, vs, kw, vw, gl, zg, zr, zu, zv = [w[:, o[i]:o[i + 1]] for i in range(12)]
    wa = jnp.concatenate([q, ks, vs, kw, vw], axis=1).astype(BF16)
    pad = jnp.zeros((w.shape[0], ZB_COLS - ZB_GL - N_GATE * N_GROUP), w.dtype)
    wb = jnp.concatenate([kc, vc, zg, zr, zu, zv, gl, pad], axis=1).astype(BF16)
    return wa, wb


def kernel(x, c, w_mod, b_mod, w_in, cmp_pos, cmp_w1, cmp_b1, cmp_w2, lru_conv_w, lru_conv_b, lru_wa,
           lru_ba, lru_wx, lru_bx, lru_lambda, sgu_ln_g, sgu_ln_b, sgu_w, sgu_b, w_o, ln1_g, ln1_b,
           ffn_w_in, ffn_conv_w, ffn_conv_b, ffn_w_down, ln2_g, ln2_b):
    batch, seq, d = x.shape
    depth = w_mod.shape[0]
    m = batch * seq
    x2 = x.reshape(m, d)
    c_pad = jnp.pad(c, ((0, SUBLANE - batch), (0, 0)))
    mod = _modulation(c_pad, w_mod, b_mod).reshape(depth, SUBLANE, 1, 6 * d)
    for l in range(depth):
        mod_l = mod[l]
        wa, wb = _split_w_in(w_in[l])
        za = _modmm(x2, mod_l, 1, 0, wa, BF16, seq, name="in_proj_qkv")
        zb = _modmm(x2, mod_l, 1, 0, wb, F32, seq, name="in_proj_rest")
        cmp = _compress(zb, cmp_pos[l], cmp_w1[l], cmp_b1[l], cmp_w2[l], batch, seq)
        kc = cmp[0]
        vct = jnp.swapaxes(cmp[1], -1, -2)
        gl = zb[:, ZB_GL:ZB_GL + N_GATE * N_GROUP].reshape(m, N_GROUP, N_GATE).transpose(1, 0, 2)
        y_attn = _attention(za, kc, vct, gl, batch, seq)
        y_lru = _lru(zb, lru_conv_w[l], lru_conv_b[l], lru_wa[l], lru_ba[l], lru_wx[l], lru_bx[l],
                     lru_lambda[l], batch, seq)
        y_sgu = _sgu(zb, sgu_ln_g[l], sgu_ln_b[l], sgu_w[l], sgu_b[l], batch, seq)
        y = jnp.concatenate([y_attn, y_lru, y_sgu], axis=1)
        x2 = _oproj(y, w_o[l].astype(BF16), x2, mod_l, 2, ln1_g[l], ln1_b[l], seq)
        x2 = _ffn(x2, mod_l, 4, 3, 5, ffn_w_in[l].astype(BF16), ffn_conv_w[l], ffn_conv_b[l],
                  ffn_w_down[l].astype(BF16), ln2_g[l], ln2_b[l], seq)
    return x2.reshape(batch, seq, d)
```

```python
import functools

import jax
import jax.numpy as jnp
import numpy as np
from jax import lax
from jax.experimental import pallas as pl
from jax.experimental.pallas import tpu as pltpu

F32 = jnp.float32
BF16 = jnp.bfloat16

DEPTH = 2
HEAD_DIM = 128
HPG = 4
N_GROUP = 4
ATTN_W = HPG * N_GROUP * HEAD_DIM
KV_W = N_GROUP * HEAD_DIM
LRU_W = 1024
LRU_BLOCKS = 8
LRU_CONV = 4
LRU_C = 8.0
GMLP_W = 1024
GMLP_GROUPS = 8
CHUNK = 128
CMP_LEN = 32
CMP_STRIDE = 16
SEL_BLOCK = 64
SEL_TOPK = 16
WINDOW = 512
FFN_CONV = 3
NEG = -1e30
FORCE = 1e4
LN_EPS = 1e-5
ALPHA = (2.0 * DEPTH) ** 0.25
SCALE = HEAD_DIM ** -0.5
EXP2_SCALE = SCALE * 1.4426950408889634
N_GATE = 3 * HPG
SEL_PAD = 128

V7X_VMEM_LIMIT_BYTES = 58 * 1024 * 1024
LANE = 128
SUBLANE = 8

ZB_KC, ZB_VC, ZB_G, ZB_R, ZB_U, ZB_V, ZB_GL = 0, 512, 1024, 2048, 3072, 4096, 5120
ZB_COLS = 5632


def _params(sem):
    return pltpu.CompilerParams(dimension_semantics=sem, vmem_limit_bytes=V7X_VMEM_LIMIT_BYTES)


def _gelu(x):
    return 0.5 * x * (1.0 + jnp.tanh(0.7978845608028654 * (x + 0.044715 * (x * x * x))))


def _layer_norm(v, g, b):
    mu = jnp.mean(v, axis=-1, keepdims=True)
    d = v - mu
    var = jnp.mean(d * d, axis=-1, keepdims=True)
    return d * lax.rsqrt(var + LN_EPS) * g + b


def _shift_rows(cur, prev_tail, k):
    r = pltpu.roll(cur, k, 0)
    rp = pltpu.roll(prev_tail, k, 0)
    row = lax.broadcasted_iota(jnp.int32, prev_tail.shape, 0)
    first = jnp.where(row < k, rp, r[0:SUBLANE])
    return jnp.concatenate([first, r[SUBLANE:]], axis=0)


def _modulate_into(h_ref, x_ref, sc_ref, sh_ref, row_chunk=128):
    scale = 1.0 + sc_ref[...]
    shift = sh_ref[...]

    def body(r, _):
        rs = pl.ds(pl.multiple_of(r * row_chunk, row_chunk), row_chunk)
        h_ref[rs, :] = (x_ref[rs, :] * scale + shift).astype(h_ref.dtype)
        return 0

    lax.fori_loop(0, h_ref.shape[0] // row_chunk, body, 0)


def _zero_rows(o_ref, row_chunk=128):
    def body(r, _):
        rs = pl.ds(pl.multiple_of(r * row_chunk, row_chunk), row_chunk)
        o_ref[rs, :] = jnp.zeros((row_chunk, o_ref.shape[1]), o_ref.dtype)
        return 0

    lax.fori_loop(0, o_ref.shape[0] // row_chunk, body, 0)


def _accumulate_matmul(o_ref, lhs, w_ref, col_chunk=1024):
    n = o_ref.shape[1]
    for c0 in range(0, n, col_chunk):
        cs = slice(c0, min(c0 + col_chunk, n))
        o_ref[:, cs] += jnp.dot(lhs, w_ref[:, cs], preferred_element_type=F32)


def _deepnorm_epilogue(o_ref, x_ref, gm_ref, lng_ref, lnb_ref, row_chunk=128):
    gate = 1.0 + gm_ref[...]
    lng = lng_ref[...]
    lnb = lnb_ref[...]

    def body(r, _):
        rs = pl.ds(pl.multiple_of(r * row_chunk, row_chunk), row_chunk)
        v = ALPHA * x_ref[rs, :] + gate * o_ref[rs, :]
        o_ref[rs, :] = _layer_norm(v, lng, lnb)
        return 0

    lax.fori_loop(0, o_ref.shape[0] // row_chunk, body, 0)


def _mod_kernel(c_ref, w_ref, b_ref, o_ref):
    c = c_ref[...]
    cs = (c * jax.nn.sigmoid(c)).astype(BF16)
    o_ref[...] = jnp.dot(cs, w_ref[...].astype(BF16), preferred_element_type=F32) + b_ref[...]


def _modulation(c_pad, w_mod, b_mod, tn=512):
    depth, d, n = w_mod.shape
    rows = c_pad.shape[0]
    return pl.pallas_call(
        _mod_kernel,
        grid=(depth, n // tn),
        in_specs=[
            pl.BlockSpec((rows, d), lambda l, j: (0, 0)),
            pl.BlockSpec((None, d, tn), lambda l, j: (l, 0, j)),
            pl.BlockSpec((None, 1, tn), lambda l, j: (l, 0, j)),
        ],
        out_specs=pl.BlockSpec((None, rows, tn), lambda l, j: (l, 0, j)),
        out_shape=jax.ShapeDtypeStruct((depth, rows, n), F32),
        compiler_params=_params(("arbitrary", "arbitrary")),
        name="modulation",
    )(c_pad, w_mod, b_mod.reshape(depth, 1, n))


def _modmm_kernel(x_ref, sc_ref, sh_ref, w_ref, o_ref, h_ref):
    @pl.when(pl.program_id(1) == 0)
    def _():
        _modulate_into(h_ref, x_ref, sc_ref, sh_ref)

    o_ref[...] = jnp.dot(h_ref[...], w_ref[...], preferred_element_type=F32).astype(o_ref.dtype)


def _modmm(x, mod_l, sc_idx, sh_idx, w, out_dtype, rows_per_batch, tm=512, tn=512, name="modmm"):
    m, k = x.shape
    n = w.shape[1]
    tpb = rows_per_batch // tm
    return pl.pallas_call(
        _modmm_kernel,
        grid=(m // tm, n // tn),
        in_specs=[
            pl.BlockSpec((tm, k), lambda i, j: (i, 0)),
            pl.BlockSpec((None, 1, k), lambda i, j: (i // tpb, 0, sc_idx)),
            pl.BlockSpec((None, 1, k), lambda i, j: (i // tpb, 0, sh_idx)),
            pl.BlockSpec((k, tn), lambda i, j: (0, j)),
        ],
        out_specs=pl.BlockSpec((tm, tn), lambda i, j: (i, j)),
        out_shape=jax.ShapeDtypeStruct((m, n), out_dtype),
        scratch_shapes=[pltpu.VMEM((tm, k), BF16)],
        compiler_params=_params(("arbitrary", "arbitrary")),
        name=name,
    )(x, mod_l, mod_l, w)


def _compress_kernel(kv_ref, pos_ref, w1_ref, b1_ref, w2_ref, o_ref, *, ncp):
    a0 = jnp.zeros((ncp, HEAD_DIM), F32)
    a1 = jnp.zeros((ncp, HEAD_DIM), F32)
    for r in range(CMP_STRIDE):
        xr = kv_ref[pl.ds(r, ncp, stride=CMP_STRIDE), :]
        lo = (xr + pos_ref[r:r + 1, :]).astype(BF16)
        hi = (xr + pos_ref[CMP_STRIDE + r:CMP_STRIDE + r + 1, :]).astype(BF16)
        a0 = a0 + jnp.dot(lo, w1_ref[r * HEAD_DIM:(r + 1) * HEAD_DIM, :], preferred_element_type=F32)
        a1 = a1 + jnp.dot(hi, w1_ref[(CMP_STRIDE + r) * HEAD_DIM:(CMP_STRIDE + r + 1) * HEAD_DIM, :],
                          preferred_element_type=F32)
    pre = a0 + pltpu.roll(a1, ncp - 1, 0) + b1_ref[...]
    hid = _gelu(pre).astype(BF16)
    o_ref[...] = jnp.dot(hid, w2_ref[...], preferred_element_type=F32).astype(o_ref.dtype)


def _compress(zb, cmp_pos, cmp_w1, cmp_b1, cmp_w2, batch, seq):
    ncp = seq // CMP_STRIDE
    kern = functools.partial(_compress_kernel, ncp=ncp)
    return pl.pallas_call(
        kern,
        grid=(2, batch, N_GROUP),
        in_specs=[
            pl.BlockSpec((seq, HEAD_DIM), lambda w, b, g: (b, w * N_GROUP + g)),
            pl.BlockSpec((None, CMP_LEN, HEAD_DIM), lambda w, b, g: (w, 0, 0)),
            pl.BlockSpec((None, CMP_LEN * HEAD_DIM, HEAD_DIM), lambda w, b, g: (w, 0, 0)),
            pl.BlockSpec((None, 1, HEAD_DIM), lambda w, b, g: (w, 0, 0)),
            pl.BlockSpec((None, HEAD_DIM, HEAD_DIM), lambda w, b, g: (w, 0, 0)),
        ],
        out_specs=pl.BlockSpec((None, None, None, ncp, HEAD_DIM), lambda w, b, g: (w, b, g, 0, 0)),
        out_shape=jax.ShapeDtypeStruct((2, batch, N_GROUP, ncp, HEAD_DIM), BF16),
        compiler_params=_params(("arbitrary", "arbitrary", "arbitrary")),
        name="nsa_compress",
    )(zb, cmp_pos, cmp_w1.astype(BF16), cmp_b1.reshape(2, 1, HEAD_DIM), cmp_w2.astype(BF16))


_NT = (((1,), (1,)), ((), ()))


def _attn_kernel(q_ref, ks_ref, vs_ref, kw_ref, vw_ref, ke_ref, kc_ref, vct_ref, cov_ref, gl_ref,
                 o_ref, qa_ref, *, tq, tk, ncp, nsel):
    half = (HPG // 2) * tq
    q0 = pl.program_id(2) * tq
    for h in range(HPG):
        qa_ref[h * tq:(h + 1) * tq, 0:HEAD_DIM] = q_ref[:, h * HEAD_DIM:(h + 1) * HEAD_DIM]

    ci = lax.broadcasted_iota(jnp.int32, (ncp, half), 0) * CMP_STRIDE + (CMP_LEN - 1)
    tc = lax.broadcasted_iota(jnp.int32, (ncp, half), 1) & (tq - 1)
    valid_c = ci - tc <= q0
    t_row = q0 + (lax.broadcasted_iota(jnp.int32, (1, half), 1) & (tq - 1))
    any_c = (t_row >= CMP_LEN - 1).astype(F32)
    o_ct, p_sum = [], None
    for c in range(2):
        qc = qa_ref[c * half:(c + 1) * half, 0:HEAD_DIM]
        s_c = lax.dot_general(kc_ref[...], qc, _NT, preferred_element_type=F32)
        s_c = jnp.where(valid_c, s_c, NEG)
        m_c = jnp.max(s_c, axis=0, keepdims=True)
        p_c = jnp.exp2((s_c - m_c) * EXP2_SCALE)
        l_c = jnp.sum(p_c, axis=0, keepdims=True)
        p_c = p_c * (any_c / l_c)
        o_ct.append(jnp.dot(vct_ref[...], p_c.astype(BF16), preferred_element_type=F32))
        for h in range(HPG // 2):
            ph = p_c[:, h * tq:(h + 1) * tq]
            p_sum = ph if p_sum is None else p_sum + ph

    imp = jnp.dot(cov_ref[...], p_sum.astype(BF16), preferred_element_type=F32)
    jb = lax.broadcasted_iota(jnp.int32, (SEL_PAD, tq), 0)
    tl = q0 + lax.broadcasted_iota(jnp.int32, (SEL_PAD, tq), 1)
    cur = lax.shift_right_logical(tl, 6)
    forced = (jb == 0) | (jb == cur) | (jb == cur - 1)
    imp = jnp.where(forced, FORCE, jnp.where(jb * SEL_BLOCK <= tl, imp, NEG))
    if nsel < SEL_PAD:
        imp = jnp.where(jb < nsel, imp, -jnp.inf)
    selb = jnp.full((SEL_PAD, tq), NEG, F32)
    for _ in range(min(SEL_TOPK, nsel)):
        mx = jnp.max(imp, axis=0, keepdims=True)
        idx = jnp.min(jnp.where(imp == mx, jb, SEL_PAD), axis=0, keepdims=True)
        hit = jb == idx
        selb = jnp.where(hit, 0.0, selb)
        imp = jnp.where(hit, -jnp.inf, imp)
    selb_t = selb.T.astype(BF16)
    for h in range(HPG):
        qa_ref[h * tq:(h + 1) * tq, HEAD_DIM:HEAD_DIM + SEL_PAD] = selb_t

    a = q0 // tk
    rel = ((lax.broadcasted_iota(jnp.int32, (half, tk), 0) & (tq - 1))
           - lax.broadcasted_iota(jnp.int32, (half, tk), 1))

    def online(carry, s, v):
        m, l, acc = carry
        mn = jnp.maximum(m, jnp.max(s, axis=1, keepdims=True))
        al = jnp.exp2((m - mn) * EXP2_SCALE)
        p = jnp.exp2((s - mn) * EXP2_SCALE)
        l = al * l + jnp.sum(p, axis=1, keepdims=True)
        acc = al * acc + jnp.dot(p.astype(BF16), v, preferred_element_type=F32)
        return mn, l, acc

    def sel_tile(kt, carry, masked):
        start = pl.multiple_of(kt * tk, tk)
        kaug = jnp.concatenate([ks_ref[pl.ds(start, tk), :], ke_ref[pl.ds(start, tk), :]], axis=1)
        v = vs_ref[pl.ds(start, tk), :]
        out = []
        for c in range(2):
            s = lax.dot_general(qa_ref[c * half:(c + 1) * half, :], kaug, _NT, preferred_element_type=F32)
            if masked:
                s = jnp.where(rel >= kt * tk - q0, s, NEG)
            out.append(online(carry[c], s, v))
        return tuple(out)

    init1 = (jnp.full((half, 1), NEG, F32), jnp.zeros((half, 1), F32), jnp.zeros((half, HEAD_DIM), F32))
    init = (init1, init1)
    carry = lax.fori_loop(0, a, lambda kt, cr: sel_tile(kt, cr, False), init)
    sel = sel_tile(a, carry, True)

    def win_tile(kt, lo, hi, carry):
        start = pl.multiple_of(jnp.maximum(kt, 0) * tk, tk)
        k = kw_ref[pl.ds(start, tk), :]
        v = vw_ref[pl.ds(start, tk), :]
        out = []
        for c in range(2):
            s = lax.dot_general(qa_ref[c * half:(c + 1) * half, 0:HEAD_DIM], k, _NT,
                                preferred_element_type=F32)
            ok = (rel >= lo) if hi is None else (rel < hi)
            out.append(online(carry[c], jnp.where(ok, s, NEG), v))
        return tuple(out)

    prev_hi = jnp.where(a > 0, WINDOW + (a - 1) * tk - q0, -tk)
    carry = win_tile(a - 1, None, prev_hi, init)
    win = win_tile(a, a * tk - q0, None, carry)

    gate = jax.nn.sigmoid(gl_ref[...])
    for h in range(HPG):
        c, hh = divmod(h, HPG // 2)
        sl = slice(hh * tq, (hh + 1) * tq)
        o_c = o_ct[c][:, sl].T
        o_s = sel[c][2][sl] * (1.0 / sel[c][1][sl])
        o_w = win[c][2][sl] * (1.0 / win[c][1][sl])
        out = (gate[:, 3 * h:3 * h + 1] * o_c + gate[:, 3 * h + 1:3 * h + 2] * o_s
               + gate[:, 3 * h + 2:3 * h + 3] * o_w)
        o_ref[:, h * HEAD_DIM:(h + 1) * HEAD_DIM] = out.astype(o_ref.dtype)


def _selection_constants(seq):
    ncp = seq // CMP_STRIDE
    nc = ncp - CMP_LEN // CMP_STRIDE + 1
    nsel = seq // SEL_BLOCK
    ci = np.arange(ncp)[None, :] * CMP_STRIDE
    sj = np.arange(SEL_PAD)[:, None] * SEL_BLOCK
    cover_t = ((ci < sj + SEL_BLOCK) & (ci + CMP_LEN > sj)
               & (np.arange(ncp)[None, :] < nc) & (np.arange(SEL_PAD)[:, None] < nsel))
    key_block = (np.arange(seq)[:, None] // SEL_BLOCK) == np.arange(SEL_PAD)[None, :]
    return jnp.asarray(cover_t, BF16), jnp.asarray(key_block, BF16)


def _attention(za, kc, vct, gl, batch, seq, tq=256, tk=512):
    assert tk == WINDOW and tk % tq == 0 and tq & (tq - 1) == 0 and seq % tk == 0
    ncp = seq // CMP_STRIDE
    nsel = seq // SEL_BLOCK
    nq = seq // tq
    cover_t, key_block = _selection_constants(seq)
    qcols = ATTN_W // LANE
    kern = functools.partial(_attn_kernel, tq=tq, tk=tk, ncp=ncp, nsel=nsel)
    kv_spec = lambda off: pl.BlockSpec((seq, HEAD_DIM), lambda b, g, i: (b, qcols + off * N_GROUP + g))
    return pl.pallas_call(
        kern,
        grid=(batch, N_GROUP, nq),
        in_specs=[
            pl.BlockSpec((tq, HPG * HEAD_DIM), lambda b, g, i: (b * nq + i, g)),
            kv_spec(0), kv_spec(1), kv_spec(2), kv_spec(3),
            pl.BlockSpec((seq, SEL_PAD), lambda b, g, i: (0, 0)),
            pl.BlockSpec((None, None, ncp, HEAD_DIM), lambda b, g, i: (b, g, 0, 0)),
            pl.BlockSpec((None, None, HEAD_DIM, ncp), lambda b, g, i: (b, g, 0, 0)),
            pl.BlockSpec((SEL_PAD, ncp), lambda b, g, i: (0, 0)),
            pl.BlockSpec((None, tq, N_GATE), lambda b, g, i: (g, b * nq + i, 0)),
        ],
        out_specs=pl.BlockSpec((tq, HPG * HEAD_DIM), lambda b, g, i: (b * nq + i, g)),
        out_shape=jax.ShapeDtypeStruct((batch * seq, ATTN_W), BF16),
        scratch_shapes=[pltpu.VMEM((HPG * tq, HEAD_DIM + SEL_PAD), BF16)],
        compiler_params=_params(("arbitrary", "arbitrary", "arbitrary")),
        name="nsa_attention",
    )(za, za, za, za, za, key_block, kc, vct, cover_t, gl)


def _lru_kernel(zg_ref, zr_ref, cw_ref, cb_ref, wa_ref, ba_ref, wx_ref, bx_ref, lam_ref, o_ref,
                tail_ref, h_ref, *, tt):
    @pl.when(pl.program_id(1) == 0)
    def _():
        tail_ref[...] = jnp.zeros_like(tail_ref)
        h_ref[...] = jnp.zeros_like(h_ref)

    zr = zr_ref[...]
    width = zr.shape[1]
    prev = tail_ref[...]
    xr = cw_ref[LRU_CONV - 1:LRU_CONV, :] * zr + cb_ref[...]
    for k in range(1, LRU_CONV):
        xr = xr + cw_ref[LRU_CONV - 1 - k:LRU_CONV - k, :] * _shift_rows(zr, prev, k)
    tail_ref[...] = zr[tt - SUBLANE:tt]

    xb = xr.astype(BF16)
    bw = width // LRU_BLOCKS
    r_parts, i_parts = [], []
    for n in range(LRU_BLOCKS):
        xs = xb[:, n * bw:(n + 1) * bw]
        r_parts.append(jnp.dot(xs, wa_ref[n], preferred_element_type=F32))
        i_parts.append(jnp.dot(xs, wx_ref[n], preferred_element_type=F32))
    rg = jax.nn.sigmoid(jnp.concatenate(r_parts, axis=1) + ba_ref[...])
    ig = jax.nn.sigmoid(jnp.concatenate(i_parts, axis=1) + bx_ref[...])
    lam = lam_ref[...]
    softplus_neg = jnp.maximum(-lam, 0.0) + jnp.log1p(jnp.exp(-jnp.abs(lam)))
    log_a = (-LRU_C) * rg * softplus_neg
    a = jnp.exp(log_a)
    b = jnp.sqrt(1.0 - a * a) * (ig * xr)

    row = lax.broadcasted_iota(jnp.int32, (tt, width), 0)
    d = 1
    while d < tt:
        a_s = pltpu.roll(a, d, 0)
        b_s = pltpu.roll(b, d, 0)
        keep = row >= d
        b = jnp.where(keep, a * b_s + b, b)
        a = jnp.where(keep, a * a_s, a)
        d *= 2
    h = a * h_ref[...] + b
    h_ref[...] = h[tt - 1:tt]
    o_ref[...] = (_gelu(zg_ref[...]) * h).astype(o_ref.dtype)


def _lru(zb, conv_w, conv_b, wa, ba, wx, bx, lam, batch, seq, tt=256):
    nt = seq // tt
    cg, cr = ZB_G // LRU_W, ZB_R // LRU_W
    kern = functools.partial(_lru_kernel, tt=tt)
    vec = lambda: pl.BlockSpec((1, LRU_W), lambda b, t: (0, 0))
    mat = lambda: pl.BlockSpec((LRU_BLOCKS, LRU_W // LRU_BLOCKS, LRU_W // LRU_BLOCKS), lambda b, t: (0, 0, 0))
    return pl.pallas_call(
        kern,
        grid=(batch, nt),
        in_specs=[
            pl.BlockSpec((tt, LRU_W), lambda b, t: (b * nt + t, cg)),
            pl.BlockSpec((tt, LRU_W), lambda b, t: (b * nt + t, cr)),
            pl.BlockSpec((LRU_CONV, LRU_W), lambda b, t: (0, 0)),
            vec(), mat(), vec(), mat(), vec(), vec(),
        ],
        out_specs=pl.BlockSpec((tt, LRU_W), lambda b, t: (b * nt + t, 0)),
        out_shape=jax.ShapeDtypeStruct((batch * seq, LRU_W), BF16),
        scratch_shapes=[pltpu.VMEM((SUBLANE, LRU_W), F32), pltpu.VMEM((1, LRU_W), F32)],
        compiler_params=_params(("arbitrary", "arbitrary")),
        name="rg_lru",
    )(zb, zb, conv_w, conv_b.reshape(1, LRU_W), wa.astype(BF16), ba.reshape(1, LRU_W),
      wx.astype(BF16), bx.reshape(1, LRU_W), lam.reshape(1, LRU_W))


def _sgu_kernel(zu_ref, zv_ref, g_ref, b_ref, ws_ref, bst_ref, o_ref, *, nchunk):
    v = _layer_norm(_gelu(zv_ref[...]), g_ref[...], b_ref[...]).astype(BF16)
    u = _gelu(zu_ref[...])
    gw = GMLP_W // GMLP_GROUPS
    tril = (lax.broadcasted_iota(jnp.int32, (CHUNK, CHUNK), 0)
            >= lax.broadcasted_iota(jnp.int32, (CHUNK, CHUNK), 1))
    for gi in range(GMLP_GROUPS):
        cs = slice(gi * gw, (gi + 1) * gw)
        w = jnp.where(tril, ws_ref[gi], 0.0).astype(BF16)
        rhs = jnp.concatenate([v[c * CHUNK:(c + 1) * CHUNK, cs] for c in range(nchunk)], axis=1)
        y = jnp.dot(w, rhs, preferred_element_type=F32) + bst_ref[:, gi:gi + 1]
        for c in range(nchunk):
            rs = slice(c * CHUNK, (c + 1) * CHUNK)
            o_ref[rs, cs] = (u[rs, cs] * y[:, c * gw:(c + 1) * gw]).astype(o_ref.dtype)


def _sgu(zb, ln_g, ln_b, w_s, b_s, batch, seq, nchunk=4):
    tt = nchunk * CHUNK
    nt = batch * seq // tt
    cu, cv = ZB_U // GMLP_W, ZB_V // GMLP_W
    kern = functools.partial(_sgu_kernel, nchunk=nchunk)
    return pl.pallas_call(
        kern,
        grid=(nt,),
        in_specs=[
            pl.BlockSpec((tt, GMLP_W), lambda t: (t, cu)),
            pl.BlockSpec((tt, GMLP_W), lambda t: (t, cv)),
            pl.BlockSpec((1, GMLP_W), lambda t: (0, 0)),
            pl.BlockSpec((1, GMLP_W), lambda t: (0, 0)),
            pl.BlockSpec((GMLP_GROUPS, CHUNK, CHUNK), lambda t: (0, 0, 0)),
            pl.BlockSpec((CHUNK, GMLP_GROUPS), lambda t: (0, 0)),
        ],
        out_specs=pl.BlockSpec((tt, GMLP_W), lambda t: (t, 0)),
        out_shape=jax.ShapeDtypeStruct((batch * seq, GMLP_W), BF16),
        compiler_params=_params(("arbitrary",)),
        name="spatial_gating",
    )(zb, zb, ln_g.reshape(1, GMLP_W), ln_b.reshape(1, GMLP_W), w_s, b_s.T)


def _oproj_kernel(y_ref, w_ref, x_ref, gm_ref, lng_ref, lnb_ref, o_ref):
    k = pl.program_id(1)

    @pl.when(k == 0)
    def _():
        _zero_rows(o_ref)

    _accumulate_matmul(o_ref, y_ref[...], w_ref)

    @pl.when(k == pl.num_programs(1) - 1)
    def _():
        _deepnorm_epilogue(o_ref, x_ref, gm_ref, lng_ref, lnb_ref)


def _oproj(y, w, x, mod_l, gate_idx, ln_g, ln_b, rows_per_batch, tm=512, tk=1024):
    m, kdim = y.shape
    d = w.shape[1]
    tpb = rows_per_batch // tm
    return pl.pallas_call(
        _oproj_kernel,
        grid=(m // tm, kdim // tk),
        in_specs=[
            pl.BlockSpec((tm, tk), lambda i, k: (i, k)),
            pl.BlockSpec((tk, d), lambda i, k: (k, 0)),
            pl.BlockSpec((tm, d), lambda i, k: (i, 0), pipeline_mode=pl.Buffered(1)),
            pl.BlockSpec((None, 1, d), lambda i, k: (i // tpb, 0, gate_idx)),
            pl.BlockSpec((1, d), lambda i, k: (0, 0)),
            pl.BlockSpec((1, d), lambda i, k: (0, 0)),
        ],
        out_specs=pl.BlockSpec((tm, d), lambda i, k: (i, 0)),
        out_shape=jax.ShapeDtypeStruct((m, d), F32),
        compiler_params=_params(("arbitrary", "arbitrary")),
        name="out_proj_ln",
    )(y, w, x, mod_l, ln_g.reshape(1, d), ln_b.reshape(1, d))


def _ffn_kernel(x_ref, sc_ref, sh_ref, gm_ref, wg_ref, wu_ref, cw_ref, cb_ref, wd_ref, lng_ref, lnb_ref,
                o_ref, h_ref, tail_ref, act_ref, *, tm, tpb, nf):
    i = pl.program_id(0)
    f = pl.program_id(1)

    @pl.when(f == 0)
    def _():
        _modulate_into(h_ref, x_ref, sc_ref, sh_ref)
        _zero_rows(o_ref)
        act_ref[1] = jnp.zeros(act_ref.shape[1:], act_ref.dtype)

    @pl.when((f == 0) & (i == 0))
    def _():
        tail_ref[...] = jnp.zeros(tail_ref.shape, tail_ref.dtype)

    fb = jnp.minimum(f, nf - 1)
    h = h_ref[...]
    g = jnp.dot(h, wg_ref[...], preferred_element_type=F32)
    u = jnp.dot(h, wu_ref[...], preferred_element_type=F32)
    _accumulate_matmul(o_ref, act_ref[(f + 1) % 2], wd_ref)

    prev = jnp.where(i % tpb == 0, 0.0, tail_ref[fb])
    tail_ref[fb] = g[tm - SUBLANE:tm]
    gc = cw_ref[FFN_CONV - 1:FFN_CONV, :] * g + cb_ref[...]
    for k in range(1, FFN_CONV):
        gc = gc + cw_ref[FFN_CONV - 1 - k:FFN_CONV - k, :] * _shift_rows(g, prev, k)
    act_ref[f % 2] = (gc * jax.nn.sigmoid(gc) * u).astype(BF16)

    @pl.when(f == nf)
    def _():
        _deepnorm_epilogue(o_ref, x_ref, gm_ref, lng_ref, lnb_ref)


def _ffn(x, mod_l, sc_idx, sh_idx, gate_idx, w_in, conv_w, conv_b, w_down, ln_g, ln_b, rows_per_batch,
         tm=512, tf=256):
    m, d = x.shape
    dff = w_down.shape[0]
    nf = dff // tf
    tpb = rows_per_batch // tm
    kern = functools.partial(_ffn_kernel, tm=tm, tpb=tpb, nf=nf)
    cur = lambda f: jnp.minimum(f, nf - 1)
    prv = lambda f: jnp.maximum(f - 1, 0)
    return pl.pallas_call(
        kern,
        grid=(m // tm, nf + 1),
        in_specs=[
            pl.BlockSpec((tm, d), lambda i, f: (i, 0), pipeline_mode=pl.Buffered(1)),
            pl.BlockSpec((None, 1, d), lambda i, f: (i // tpb, 0, sc_idx)),
            pl.BlockSpec((None, 1, d), lambda i, f: (i // tpb, 0, sh_idx)),
            pl.BlockSpec((None, 1, d), lambda i, f: (i // tpb, 0, gate_idx)),
            pl.BlockSpec((d, tf), lambda i, f: (0, cur(f))),
            pl.BlockSpec((d, tf), lambda i, f: (0, nf + cur(f))),
            pl.BlockSpec((FFN_CONV, tf), lambda i, f: (0, cur(f))),
            pl.BlockSpec((1, tf), lambda i, f: (0, cur(f))),
            pl.BlockSpec((tf, d), lambda i, f: (prv(f), 0)),
            pl.BlockSpec((1, d), lambda i, f: (0, 0)),
            pl.BlockSpec((1, d), lambda i, f: (0, 0)),
        ],
        out_specs=pl.BlockSpec((tm, d), lambda i, f: (i, 0)),
        out_shape=jax.ShapeDtypeStruct((m, d), F32),
        scratch_shapes=[pltpu.VMEM((tm, d), BF16), pltpu.VMEM((nf, SUBLANE, tf), F32),
                        pltpu.VMEM((2, tm, tf), BF16)],
        compiler_params=_params(("arbitrary", "arbitrary")),
        name="conv_ffn_ln",
    )(x, mod_l, mod_l, mod_l, w_in, w_in, conv_w, conv_b.reshape(1, dff), w_down,
      ln_g.reshape(1, d), ln_b.reshape(1, d))


def _split_w_in(w):
    o = np.cumsum([0, ATTN_W] + [KV_W] * 6 + [N_GATE * N_GROUP, LRU_W, LRU_W, GMLP_W, GMLP_W]).tolist()
    q, kc, vc, ks, vs, kw, vw, gl, zg, zr, zu, zv = [w[:, o[i]:o[i + 1]] for i in range(12)]
    wa = jnp.concatenate([q, ks, vs, kw, vw], axis=1).astype(BF16)
    pad = jnp.zeros((w.shape[0], ZB_COLS - ZB_GL - N_GATE * N_GROUP), w.dtype)
    wb = jnp.concatenate([kc, vc, zg, zr, zu, zv, gl, pad], axis=1).astype(BF16)
    return wa, wb


def kernel(x, c, w_mod, b_mod, w_in, cmp_pos, cmp_w1, cmp_b1, cmp_w2, lru_conv_w, lru_conv_b, lru_wa,
           lru_ba, lru_wx, lru_bx, lru_lambda, sgu_ln_g, sgu_ln_b, sgu_w, sgu_b, w_o, ln1_g, ln1_b,
           ffn_w_in, ffn_conv_w, ffn_conv_b, ffn_w_down, ln2_g, ln2_b):
    batch, seq, d = x.shape
    depth = w_mod.shape[0]
    m = batch * seq
    x2 = x.reshape(m, d)
    c_pad = jnp.pad(c, ((0, SUBLANE - batch), (0, 0)))
    mod = _modulation(c_pad, w_mod, b_mod).reshape(depth, SUBLANE, 1, 6 * d)
    for l in range(depth):
        mod_l = mod[l]
        wa, wb = _split_w_in(w_in[l])
        za = _modmm(x2, mod_l, 1, 0, wa, BF16, seq, tn=1024, name="in_proj_qkv")
        zb = _modmm(x2, mod_l, 1, 0, wb, F32, seq, name="in_proj_rest")
        cmp = _compress(zb, cmp_pos[l], cmp_w1[l], cmp_b1[l], cmp_w2[l], batch, seq)
        kc = cmp[0]
        vct = jnp.swapaxes(cmp[1], -1, -2)
        gl = zb[:, ZB_GL:ZB_GL + N_GATE * N_GROUP].reshape(m, N_GROUP, N_GATE).transpose(1, 0, 2)
        y_attn = _attention(za, kc, vct, gl, batch, seq)
        y_lru = _lru(zb, lru_conv_w[l], lru_conv_b[l], lru_wa[l], lru_ba[l], lru_wx[l], lru_bx[l],
                     lru_lambda[l], batch, seq)
        y_sgu = _sgu(zb, sgu_ln_g[l], sgu_ln_b[l], sgu_w[l], sgu_b[l], batch, seq)
        y = jnp.concatenate([y_attn, y_lru, y_sgu], axis=1)
        x2 = _oproj(y, w_o[l].astype(BF16), x2, mod_l, 2, ln1_g[l], ln1_b[l], seq)
        x2 = _ffn(x2, mod_l, 4, 3, 5, ffn_w_in[l].astype(BF16), ffn_conv_w[l], ffn_conv_b[l],
                  ffn_w_down[l].astype(BF16), ln2_g[l], ln2_b[l], seq)
    return x2.reshape(batch, seq, d)
```

```python
import functools

import jax
import jax.numpy as jnp
import numpy as np
from jax import lax
from jax.experimental import pallas as pl
from jax.experimental.pallas import tpu as pltpu

F32 = jnp.float32
BF16 = jnp.bfloat16

DEPTH = 2
HEAD_DIM = 128
HPG = 4
N_GROUP = 4
ATTN_W = HPG * N_GROUP * HEAD_DIM
KV_W = N_GROUP * HEAD_DIM
LRU_W = 1024
LRU_BLOCKS = 8
LRU_CONV = 4
LRU_C = 8.0
GMLP_W = 1024
GMLP_GROUPS = 8
CHUNK = 128
CMP_LEN = 32
CMP_STRIDE = 16
SEL_BLOCK = 64
SEL_TOPK = 16
WINDOW = 512
FFN_CONV = 3
NEG = -1e30
FORCE = 1e4
LN_EPS = 1e-5
ALPHA = (2.0 * DEPTH) ** 0.25
SCALE = HEAD_DIM ** -0.5
EXP2_SCALE = SCALE * 1.4426950408889634
N_GATE = 3 * HPG
SEL_PAD = 128

V7X_VMEM_LIMIT_BYTES = 58 * 1024 * 1024
LANE = 128
SUBLANE = 8

ZB_KC, ZB_VC, ZB_G, ZB_R, ZB_U, ZB_V, ZB_GL = 0, 512, 1024, 2048, 3072, 4096, 5120
ZB_COLS = 5632


def _params(sem):
    return pltpu.CompilerParams(dimension_semantics=sem, vmem_limit_bytes=V7X_VMEM_LIMIT_BYTES)


def _gelu(x):
    return 0.5 * x * (1.0 + jnp.tanh(0.7978845608028654 * (x + 0.044715 * (x * x * x))))


def _layer_norm(v, g, b):
    mu = jnp.mean(v, axis=-1, keepdims=True)
    d = v - mu
    var = jnp.mean(d * d, axis=-1, keepdims=True)
    return d * lax.rsqrt(var + LN_EPS) * g + b


def _shift_rows(cur, prev_tail, k):
    r = pltpu.roll(cur, k, 0)
    rp = pltpu.roll(prev_tail, k, 0)
    row = lax.broadcasted_iota(jnp.int32, prev_tail.shape, 0)
    first = jnp.where(row < k, rp, r[0:SUBLANE])
    return jnp.concatenate([first, r[SUBLANE:]], axis=0)


def _modulate_into(h_ref, x_ref, sc_ref, sh_ref, row_chunk=128):
    scale = 1.0 + sc_ref[...]
    shift = sh_ref[...]

    def body(r, _):
        rs = pl.ds(pl.multiple_of(r * row_chunk, row_chunk), row_chunk)
        h_ref[rs, :] = (x_ref[rs, :] * scale + shift).astype(h_ref.dtype)
        return 0

    lax.fori_loop(0, h_ref.shape[0] // row_chunk, body, 0)


def _zero_rows(o_ref, row_chunk=128):
    def body(r, _):
        rs = pl.ds(pl.multiple_of(r * row_chunk, row_chunk), row_chunk)
        o_ref[rs, :] = jnp.zeros((row_chunk, o_ref.shape[1]), o_ref.dtype)
        return 0

    lax.fori_loop(0, o_ref.shape[0] // row_chunk, body, 0)


def _accumulate_matmul(o_ref, lhs, w_ref, col_chunk=1024):
    n = o_ref.shape[1]
    for c0 in range(0, n, col_chunk):
        cs = slice(c0, min(c0 + col_chunk, n))
        o_ref[:, cs] += jnp.dot(lhs, w_ref[:, cs], preferred_element_type=F32)


def _deepnorm_epilogue(o_ref, x_ref, gm_ref, lng_ref, lnb_ref, row_chunk=128):
    gate = 1.0 + gm_ref[...]
    lng = lng_ref[...]
    lnb = lnb_ref[...]

    def body(r, _):
        rs = pl.ds(pl.multiple_of(r * row_chunk, row_chunk), row_chunk)
        v = ALPHA * x_ref[rs, :] + gate * o_ref[rs, :]
        o_ref[rs, :] = _layer_norm(v, lng, lnb)
        return 0

    lax.fori_loop(0, o_ref.shape[0] // row_chunk, body, 0)


def _mod_kernel(c_ref, w_ref, b_ref, o_ref):
    c = c_ref[...]
    cs = (c * jax.nn.sigmoid(c)).astype(BF16)
    o_ref[...] = jnp.dot(cs, w_ref[...].astype(BF16), preferred_element_type=F32) + b_ref[...]


def _modulation(c_pad, w_mod, b_mod, tn=512):
    depth, d, n = w_mod.shape
    rows = c_pad.shape[0]
    return pl.pallas_call(
        _mod_kernel,
        grid=(depth, n // tn),
        in_specs=[
            pl.BlockSpec((rows, d), lambda l, j: (0, 0)),
            pl.BlockSpec((None, d, tn), lambda l, j: (l, 0, j)),
            pl.BlockSpec((None, 1, tn), lambda l, j: (l, 0, j)),
        ],
        out_specs=pl.BlockSpec((None, rows, tn), lambda l, j: (l, 0, j)),
        out_shape=jax.ShapeDtypeStruct((depth, rows, n), F32),
        compiler_params=_params(("arbitrary", "arbitrary")),
        name="modulation",
    )(c_pad, w_mod, b_mod.reshape(depth, 1, n))


def _in_proj_kernel(x_ref, sc_ref, sh_ref, w_ref, za_ref, zb_ref, h_ref, *, na):
    j = pl.program_id(1)

    @pl.when(j == 0)
    def _():
        _modulate_into(h_ref, x_ref, sc_ref, sh_ref)

    r = jnp.dot(h_ref[...], w_ref[...], preferred_element_type=F32)

    @pl.when(j < na)
    def _():
        za_ref[...] = r.astype(za_ref.dtype)

    @pl.when(j >= na)
    def _():
        zb_ref[...] = r.astype(zb_ref.dtype)


def _in_proj(x, mod_l, sc_idx, sh_idx, w_blocks, na, rows_per_batch, tm=1024):
    m, k = x.shape
    nb, _, tn = w_blocks.shape
    tpb = rows_per_batch // tm
    kern = functools.partial(_in_proj_kernel, na=na)
    return pl.pallas_call(
        kern,
        grid=(m // tm, nb),
        in_specs=[
            pl.BlockSpec((tm, k), lambda i, j: (i, 0)),
            pl.BlockSpec((None, 1, k), lambda i, j: (i // tpb, 0, sc_idx)),
            pl.BlockSpec((None, 1, k), lambda i, j: (i // tpb, 0, sh_idx)),
            pl.BlockSpec((None, k, tn), lambda i, j: (j, 0, 0)),
        ],
        out_specs=[
            pl.BlockSpec((tm, tn), lambda i, j: (i, jnp.minimum(j, na - 1))),
            pl.BlockSpec((tm, tn), lambda i, j: (i, jnp.maximum(j - na, 0))),
        ],
        out_shape=[jax.ShapeDtypeStruct((m, na * tn), BF16),
                   jax.ShapeDtypeStruct((m, (nb - na) * tn), F32)],
        scratch_shapes=[pltpu.VMEM((tm, k), BF16)],
        compiler_params=_params(("arbitrary", "arbitrary")),
        name="in_proj",
    )(x, mod_l, mod_l, w_blocks)


def _compress_kernel(kv_ref, pos_ref, w1_ref, b1_ref, w2_ref, o_ref, *, ncp):
    a0 = jnp.zeros((ncp, HEAD_DIM), F32)
    a1 = jnp.zeros((ncp, HEAD_DIM), F32)
    for r in range(CMP_STRIDE):
        xr = kv_ref[pl.ds(r, ncp, stride=CMP_STRIDE), :]
        lo = (xr + pos_ref[r:r + 1, :]).astype(BF16)
        hi = (xr + pos_ref[CMP_STRIDE + r:CMP_STRIDE + r + 1, :]).astype(BF16)
        a0 = a0 + jnp.dot(lo, w1_ref[r * HEAD_DIM:(r + 1) * HEAD_DIM, :], preferred_element_type=F32)
        a1 = a1 + jnp.dot(hi, w1_ref[(CMP_STRIDE + r) * HEAD_DIM:(CMP_STRIDE + r + 1) * HEAD_DIM, :],
                          preferred_element_type=F32)
    pre = a0 + pltpu.roll(a1, ncp - 1, 0) + b1_ref[...]
    hid = _gelu(pre).astype(BF16)
    o_ref[...] = jnp.dot(hid, w2_ref[...], preferred_element_type=F32).astype(o_ref.dtype)


def _compress(zb, cmp_pos, cmp_w1, cmp_b1, cmp_w2, batch, seq):
    ncp = seq // CMP_STRIDE
    kern = functools.partial(_compress_kernel, ncp=ncp)
    return pl.pallas_call(
        kern,
        grid=(2, batch, N_GROUP),
        in_specs=[
            pl.BlockSpec((seq, HEAD_DIM), lambda w, b, g: (b, w * N_GROUP + g)),
            pl.BlockSpec((None, CMP_LEN, HEAD_DIM), lambda w, b, g: (w, 0, 0)),
            pl.BlockSpec((None, CMP_LEN * HEAD_DIM, HEAD_DIM), lambda w, b, g: (w, 0, 0)),
            pl.BlockSpec((None, 1, HEAD_DIM), lambda w, b, g: (w, 0, 0)),
            pl.BlockSpec((None, HEAD_DIM, HEAD_DIM), lambda w, b, g: (w, 0, 0)),
        ],
        out_specs=pl.BlockSpec((None, None, None, ncp, HEAD_DIM), lambda w, b, g: (w, b, g, 0, 0)),
        out_shape=jax.ShapeDtypeStruct((2, batch, N_GROUP, ncp, HEAD_DIM), BF16),
        compiler_params=_params(("arbitrary", "arbitrary", "arbitrary")),
        name="nsa_compress",
    )(zb, cmp_pos, cmp_w1.astype(BF16), cmp_b1.reshape(2, 1, HEAD_DIM), cmp_w2.astype(BF16))


_NT = (((1,), (1,)), ((), ()))


def _attn_kernel(q_ref, ks_ref, vs_ref, kw_ref, vw_ref, ke_ref, kc_ref, vct_ref, cov_ref, gl_ref,
                 o_ref, qa_ref, *, tq, tk, ncp, nsel):
    half = (HPG // 2) * tq
    q0 = pl.program_id(2) * tq
    for h in range(HPG):
        qa_ref[h * tq:(h + 1) * tq, 0:HEAD_DIM] = q_ref[:, h * HEAD_DIM:(h + 1) * HEAD_DIM]

    ci = lax.broadcasted_iota(jnp.int32, (ncp, half), 0) * CMP_STRIDE + (CMP_LEN - 1)
    tc = lax.broadcasted_iota(jnp.int32, (ncp, half), 1) & (tq - 1)
    valid_c = ci - tc <= q0
    t_row = q0 + (lax.broadcasted_iota(jnp.int32, (1, half), 1) & (tq - 1))
    any_c = (t_row >= CMP_LEN - 1).astype(F32)
    o_ct, p_sum = [], None
    for c in range(2):
        qc = qa_ref[c * half:(c + 1) * half, 0:HEAD_DIM]
        s_c = lax.dot_general(kc_ref[...], qc, _NT, preferred_element_type=F32)
        s_c = jnp.where(valid_c, s_c, NEG)
        m_c = jnp.max(s_c, axis=0, keepdims=True)
        p_c = jnp.exp2((s_c - m_c) * EXP2_SCALE)
        l_c = jnp.sum(p_c, axis=0, keepdims=True)
        p_c = p_c * (any_c / l_c)
        o_ct.append(jnp.dot(vct_ref[...], p_c.astype(BF16), preferred_element_type=F32))
        for h in range(HPG // 2):
            ph = p_c[:, h * tq:(h + 1) * tq]
            p_sum = ph if p_sum is None else p_sum + ph

    a = q0 // tk
    ts = 2 * tk

    def rel(width):
        return ((lax.broadcasted_iota(jnp.int32, (half, width), 0) & (tq - 1))
                - lax.broadcasted_iota(jnp.int32, (half, width), 1))

    def online(carry, s, v):
        m, l, acc = carry
        mn = jnp.maximum(m, jnp.max(s, axis=1, keepdims=True))
        al = jnp.exp2((m - mn) * EXP2_SCALE)
        p = jnp.exp2((s - mn) * EXP2_SCALE)
        l = al * l + jnp.sum(p, axis=1, keepdims=True)
        acc = al * acc + jnp.dot(p.astype(BF16), v, preferred_element_type=F32)
        return mn, l, acc

    init1 = (jnp.full((half, 1), NEG, F32), jnp.zeros((half, 1), F32), jnp.zeros((half, HEAD_DIM), F32))

    imp = jnp.dot(cov_ref[...], p_sum.astype(BF16), preferred_element_type=F32)
    jb = lax.broadcasted_iota(jnp.int32, (SEL_PAD, tq), 0)
    tl = q0 + lax.broadcasted_iota(jnp.int32, (SEL_PAD, tq), 1)
    cur = lax.shift_right_logical(tl, 6)
    forced = (jb == 0) | (jb == cur) | (jb == cur - 1)
    imp = jnp.where(forced, FORCE, jnp.where(jb * SEL_BLOCK <= tl, imp, NEG))
    if nsel < SEL_PAD:
        imp = jnp.where(jb < nsel, imp, -jnp.inf)
    selb = jnp.full((SEL_PAD, tq), NEG, F32)
    for _ in range(min(SEL_TOPK, nsel)):
        mx = jnp.max(imp, axis=0, keepdims=True)
        idx = jnp.min(jnp.where(imp == mx, jb, SEL_PAD), axis=0, keepdims=True)
        hit = jb == idx
        selb = jnp.where(hit, 0.0, selb)
        imp = jnp.where(hit, -jnp.inf, imp)
    selb_t = selb.T.astype(BF16)
    for h in range(HPG):
        qa_ref[h * tq:(h + 1) * tq, HEAD_DIM:HEAD_DIM + SEL_PAD] = selb_t

    def sel_tile(start, width, carry, causal):
        kaug = jnp.concatenate([ks_ref[pl.ds(start, width), :], ke_ref[pl.ds(start, width), :]], axis=1)
        v = vs_ref[pl.ds(start, width), :]
        out = []
        for c in range(2):
            s = lax.dot_general(qa_ref[c * half:(c + 1) * half, :], kaug, _NT, preferred_element_type=F32)
            if causal:
                s = jnp.where(rel(width) >= start - q0, s, NEG)
            out.append(online(carry[c], s, v))
        return tuple(out)

    init = (init1, init1)
    n_full = a // 2
    carry = lax.fori_loop(0, n_full, lambda kt, cr: sel_tile(pl.multiple_of(kt * ts, ts), ts, cr, False),
                          init)
    sel = sel_tile(pl.multiple_of(n_full * ts, ts), ts, carry, True)

    w0 = pl.multiple_of(jnp.maximum(a - 1, 0) * tk, tk)
    kw = kw_ref[pl.ds(w0, ts), :]
    vw = vw_ref[pl.ds(w0, ts), :]
    rel_w = rel(ts)
    win = []
    for c in range(2):
        s = lax.dot_general(qa_ref[c * half:(c + 1) * half, 0:HEAD_DIM], kw, _NT, preferred_element_type=F32)
        s = jnp.where(rel_w >= w0 - q0, jnp.where(rel_w < WINDOW + w0 - q0, s, NEG), NEG)
        win.append(online(init1, s, vw))

    gate = jax.nn.sigmoid(gl_ref[...])
    for h in range(HPG):
        c, hh = divmod(h, HPG // 2)
        sl = slice(hh * tq, (hh + 1) * tq)
        o_c = o_ct[c][:, sl].T
        o_s = sel[c][2][sl] * (1.0 / sel[c][1][sl])
        o_w = win[c][2][sl] * (1.0 / win[c][1][sl])
        out = (gate[:, 3 * h:3 * h + 1] * o_c + gate[:, 3 * h + 1:3 * h + 2] * o_s
               + gate[:, 3 * h + 2:3 * h + 3] * o_w)
        o_ref[:, h * HEAD_DIM:(h + 1) * HEAD_DIM] = out.astype(o_ref.dtype)


def _selection_constants(seq):
    ncp = seq // CMP_STRIDE
    nc = ncp - CMP_LEN // CMP_STRIDE + 1
    nsel = seq // SEL_BLOCK
    ci = np.arange(ncp)[None, :] * CMP_STRIDE
    sj = np.arange(SEL_PAD)[:, None] * SEL_BLOCK
    cover_t = ((ci < sj + SEL_BLOCK) & (ci + CMP_LEN > sj)
               & (np.arange(ncp)[None, :] < nc) & (np.arange(SEL_PAD)[:, None] < nsel))
    key_block = (np.arange(seq)[:, None] // SEL_BLOCK) == np.arange(SEL_PAD)[None, :]
    return jnp.asarray(cover_t, BF16), jnp.asarray(key_block, BF16)


def _attention(za, kc, vct, gl, batch, seq, tq=256, tk=512):
    assert tk == WINDOW and tk % tq == 0 and tq & (tq - 1) == 0 and seq % (2 * tk) == 0
    ncp = seq // CMP_STRIDE
    nsel = seq // SEL_BLOCK
    nq = seq // tq
    cover_t, key_block = _selection_constants(seq)
    qcols = ATTN_W // LANE
    kern = functools.partial(_attn_kernel, tq=tq, tk=tk, ncp=ncp, nsel=nsel)
    kv_spec = lambda off: pl.BlockSpec((seq, HEAD_DIM), lambda b, g, i: (b, qcols + off * N_GROUP + g))
    return pl.pallas_call(
        kern,
        grid=(batch, N_GROUP, nq),
        in_specs=[
            pl.BlockSpec((tq, HPG * HEAD_DIM), lambda b, g, i: (b * nq + i, g)),
            kv_spec(0), kv_spec(1), kv_spec(2), kv_spec(3),
            pl.BlockSpec((seq, SEL_PAD), lambda b, g, i: (0, 0)),
            pl.BlockSpec((None, None, ncp, HEAD_DIM), lambda b, g, i: (b, g, 0, 0)),
            pl.BlockSpec((None, None, HEAD_DIM, ncp), lambda b, g, i: (b, g, 0, 0)),
            pl.BlockSpec((SEL_PAD, ncp), lambda b, g, i: (0, 0)),
            pl.BlockSpec((None, tq, N_GATE), lambda b, g, i: (g, b * nq + i, 0)),
        ],
        out_specs=pl.BlockSpec((tq, HPG * HEAD_DIM), lambda b, g, i: (b * nq + i, g)),
        out_shape=jax.ShapeDtypeStruct((batch * seq, ATTN_W), BF16),
        scratch_shapes=[pltpu.VMEM((HPG * tq, HEAD_DIM + SEL_PAD), BF16)],
        compiler_params=_params(("arbitrary", "arbitrary", "arbitrary")),
        name="nsa_attention",
    )(za, za, za, za, za, key_block, kc, vct, cover_t, gl)


def _lru_kernel(zg_ref, zr_ref, cw_ref, cb_ref, wa_ref, ba_ref, wx_ref, bx_ref, lam_ref, o_ref,
                tail_ref, h_ref, *, tt):
    @pl.when(pl.program_id(1) == 0)
    def _():
        tail_ref[...] = jnp.zeros_like(tail_ref)
        h_ref[...] = jnp.zeros_like(h_ref)

    zr = zr_ref[...]
    width = zr.shape[1]
    prev = tail_ref[...]
    xr = cw_ref[LRU_CONV - 1:LRU_CONV, :] * zr + cb_ref[...]
    for k in range(1, LRU_CONV):
        xr = xr + cw_ref[LRU_CONV - 1 - k:LRU_CONV - k, :] * _shift_rows(zr, prev, k)
    tail_ref[...] = zr[tt - SUBLANE:tt]

    xb = xr.astype(BF16)
    bw = width // LRU_BLOCKS
    r_parts, i_parts = [], []
    for n in range(LRU_BLOCKS):
        xs = xb[:, n * bw:(n + 1) * bw]
        r_parts.append(jnp.dot(xs, wa_ref[n], preferred_element_type=F32))
        i_parts.append(jnp.dot(xs, wx_ref[n], preferred_element_type=F32))
    rg = jax.nn.sigmoid(jnp.concatenate(r_parts, axis=1) + ba_ref[...])
    ig = jax.nn.sigmoid(jnp.concatenate(i_parts, axis=1) + bx_ref[...])
    lam = lam_ref[...]
    softplus_neg = jnp.maximum(-lam, 0.0) + jnp.log1p(jnp.exp(-jnp.abs(lam)))
    log_a = (-LRU_C) * rg * softplus_neg
    a = jnp.exp(log_a)
    b = jnp.sqrt(1.0 - a * a) * (ig * xr)

    row = lax.broadcasted_iota(jnp.int32, (tt, width), 0)
    d = 1
    while d < tt:
        a_s = pltpu.roll(a, d, 0)
        b_s = pltpu.roll(b, d, 0)
        keep = row >= d
        b = jnp.where(keep, a * b_s + b, b)
        a = jnp.where(keep, a * a_s, a)
        d *= 2
    h = a * h_ref[...] + b
    h_ref[...] = h[tt - 1:tt]
    o_ref[...] = (_gelu(zg_ref[...]) * h).astype(o_ref.dtype)


def _lru(zb, conv_w, conv_b, wa, ba, wx, bx, lam, batch, seq, tt=256):
    nt = seq // tt
    cg, cr = ZB_G // LRU_W, ZB_R // LRU_W
    kern = functools.partial(_lru_kernel, tt=tt)
    vec = lambda: pl.BlockSpec((1, LRU_W), lambda b, t: (0, 0))
    mat = lambda: pl.BlockSpec((LRU_BLOCKS, LRU_W // LRU_BLOCKS, LRU_W // LRU_BLOCKS), lambda b, t: (0, 0, 0))
    return pl.pallas_call(
        kern,
        grid=(batch, nt),
        in_specs=[
            pl.BlockSpec((tt, LRU_W), lambda b, t: (b * nt + t, cg)),
            pl.BlockSpec((tt, LRU_W), lambda b, t: (b * nt + t, cr)),
            pl.BlockSpec((LRU_CONV, LRU_W), lambda b, t: (0, 0)),
            vec(), mat(), vec(), mat(), vec(), vec(),
        ],
        out_specs=pl.BlockSpec((tt, LRU_W), lambda b, t: (b * nt + t, 0)),
        out_shape=jax.ShapeDtypeStruct((batch * seq, LRU_W), BF16),
        scratch_shapes=[pltpu.VMEM((SUBLANE, LRU_W), F32), pltpu.VMEM((1, LRU_W), F32)],
        compiler_params=_params(("arbitrary", "arbitrary")),
        name="rg_lru",
    )(zb, zb, conv_w, conv_b.reshape(1, LRU_W), wa.astype(BF16), ba.reshape(1, LRU_W),
      wx.astype(BF16), bx.reshape(1, LRU_W), lam.reshape(1, LRU_W))


def _sgu_kernel(zu_ref, zv_ref, g_ref, b_ref, ws_ref, bst_ref, o_ref, *, nchunk):
    v = _layer_norm(_gelu(zv_ref[...]), g_ref[...], b_ref[...]).astype(BF16)
    u = _gelu(zu_ref[...])
    gw = GMLP_W // GMLP_GROUPS
    tril = (lax.broadcasted_iota(jnp.int32, (CHUNK, CHUNK), 0)
            >= lax.broadcasted_iota(jnp.int32, (CHUNK, CHUNK), 1))
    for gi in range(GMLP_GROUPS):
        cs = slice(gi * gw, (gi + 1) * gw)
        w = jnp.where(tril, ws_ref[gi], 0.0).astype(BF16)
        rhs = jnp.concatenate([v[c * CHUNK:(c + 1) * CHUNK, cs] for c in range(nchunk)], axis=1)
        y = jnp.dot(w, rhs, preferred_element_type=F32) + bst_ref[:, gi:gi + 1]
        for c in range(nchunk):
            rs = slice(c * CHUNK, (c + 1) * CHUNK)
            o_ref[rs, cs] = (u[rs, cs] * y[:, c * gw:(c + 1) * gw]).astype(o_ref.dtype)


def _sgu(zb, ln_g, ln_b, w_s, b_s, batch, seq, nchunk=4):
    tt = nchunk * CHUNK
    nt = batch * seq // tt
    cu, cv = ZB_U // GMLP_W, ZB_V // GMLP_W
    kern = functools.partial(_sgu_kernel, nchunk=nchunk)
    return pl.pallas_call(
        kern,
        grid=(nt,),
        in_specs=[
            pl.BlockSpec((tt, GMLP_W), lambda t: (t, cu)),
            pl.BlockSpec((tt, GMLP_W), lambda t: (t, cv)),
            pl.BlockSpec((1, GMLP_W), lambda t: (0, 0)),
            pl.BlockSpec((1, GMLP_W), lambda t: (0, 0)),
            pl.BlockSpec((GMLP_GROUPS, CHUNK, CHUNK), lambda t: (0, 0, 0)),
            pl.BlockSpec((CHUNK, GMLP_GROUPS), lambda t: (0, 0)),
        ],
        out_specs=pl.BlockSpec((tt, GMLP_W), lambda t: (t, 0)),
        out_shape=jax.ShapeDtypeStruct((batch * seq, GMLP_W), BF16),
        compiler_params=_params(("arbitrary",)),
        name="spatial_gating",
    )(zb, zb, ln_g.reshape(1, GMLP_W), ln_b.reshape(1, GMLP_W), w_s, b_s.T)


def _oproj_kernel(y_ref, w_ref, x_ref, gm_ref, lng_ref, lnb_ref, o_ref):
    k = pl.program_id(1)

    @pl.when(k == 0)
    def _():
        _zero_rows(o_ref)

    _accumulate_matmul(o_ref, y_ref[...], w_ref)

    @pl.when(k == pl.num_programs(1) - 1)
    def _():
        _deepnorm_epilogue(o_ref, x_ref, gm_ref, lng_ref, lnb_ref)


def _oproj(y, w, x, mod_l, gate_idx, ln_g, ln_b, rows_per_batch, tm=512, tk=1024):
    m, kdim = y.shape
    d = w.shape[1]
    tpb = rows_per_batch // tm
    return pl.pallas_call(
        _oproj_kernel,
        grid=(m // tm, kdim // tk),
        in_specs=[
            pl.BlockSpec((tm, tk), lambda i, k: (i, k)),
            pl.BlockSpec((tk, d), lambda i, k: (k, 0)),
            pl.BlockSpec((tm, d), lambda i, k: (i, 0), pipeline_mode=pl.Buffered(1)),
            pl.BlockSpec((None, 1, d), lambda i, k: (i // tpb, 0, gate_idx)),
            pl.BlockSpec((1, d), lambda i, k: (0, 0)),
            pl.BlockSpec((1, d), lambda i, k: (0, 0)),
        ],
        out_specs=pl.BlockSpec((tm, d), lambda i, k: (i, 0)),
        out_shape=jax.ShapeDtypeStruct((m, d), F32),
        compiler_params=_params(("arbitrary", "arbitrary")),
        name="out_proj_ln",
    )(y, w, x, mod_l, ln_g.reshape(1, d), ln_b.reshape(1, d))


def _ffn_kernel(x_ref, sc_ref, sh_ref, gm_ref, wg_ref, wu_ref, cw_ref, cb_ref, wd_ref, lng_ref, lnb_ref,
                o_ref, h_ref, tail_ref, act_ref, *, tm, tpb, nf):
    i = pl.program_id(0)
    f = pl.program_id(1)

    @pl.when(f == 0)
    def _():
        _modulate_into(h_ref, x_ref, sc_ref, sh_ref)
        _zero_rows(o_ref)
        act_ref[1] = jnp.zeros(act_ref.shape[1:], act_ref.dtype)

    @pl.when((f == 0) & (i == 0))
    def _():
        tail_ref[...] = jnp.zeros(tail_ref.shape, tail_ref.dtype)

    fb = jnp.minimum(f, nf - 1)
    h = h_ref[...]
    g = jnp.dot(h, wg_ref[...], preferred_element_type=F32)
    u = jnp.dot(h, wu_ref[...], preferred_element_type=F32)
    _accumulate_matmul(o_ref, act_ref[(f + 1) % 2], wd_ref)

    prev = jnp.where(i % tpb == 0, 0.0, tail_ref[fb])
    tail_ref[fb] = g[tm - SUBLANE:tm]
    gc = cw_ref[FFN_CONV - 1:FFN_CONV, :] * g + cb_ref[...]
    for k in range(1, FFN_CONV):
        gc = gc + cw_ref[FFN_CONV - 1 - k:FFN_CONV - k, :] * _shift_rows(g, prev, k)
    act_ref[f % 2] = (gc * jax.nn.sigmoid(gc) * u).astype(BF16)

    @pl.when(f == nf)
    def _():
        _deepnorm_epilogue(o_ref, x_ref, gm_ref, lng_ref, lnb_ref)


def _ffn(x, mod_l, sc_idx, sh_idx, gate_idx, w_in, conv_w, conv_b, w_down, ln_g, ln_b, rows_per_batch,
         tm=512):
    m, d = x.shape
    dff = w_down.shape[0]
    tf = w_in.shape[2]
    nf = dff // tf
    tpb = rows_per_batch // tm
    kern = functools.partial(_ffn_kernel, tm=tm, tpb=tpb, nf=nf)
    cur = lambda f: jnp.minimum(f, nf - 1)
    prv = lambda f: jnp.maximum(f - 1, 0)
    return pl.pallas_call(
        kern,
        grid=(m // tm, nf + 1),
        in_specs=[
            pl.BlockSpec((tm, d), lambda i, f: (i, 0), pipeline_mode=pl.Buffered(1)),
            pl.BlockSpec((None, 1, d), lambda i, f: (i // tpb, 0, sc_idx)),
            pl.BlockSpec((None, 1, d), lambda i, f: (i // tpb, 0, sh_idx)),
            pl.BlockSpec((None, 1, d), lambda i, f: (i // tpb, 0, gate_idx)),
            pl.BlockSpec((None, d, tf), lambda i, f: (cur(f), 0, 0)),
            pl.BlockSpec((None, d, tf), lambda i, f: (nf + cur(f), 0, 0)),
            pl.BlockSpec((FFN_CONV, tf), lambda i, f: (0, cur(f))),
            pl.BlockSpec((1, tf), lambda i, f: (0, cur(f))),
            pl.BlockSpec((tf, d), lambda i, f: (prv(f), 0)),
            pl.BlockSpec((1, d), lambda i, f: (0, 0)),
            pl.BlockSpec((1, d), lambda i, f: (0, 0)),
        ],
        out_specs=pl.BlockSpec((tm, d), lambda i, f: (i, 0)),
        out_shape=jax.ShapeDtypeStruct((m, d), F32),
        scratch_shapes=[pltpu.VMEM((tm, d), BF16), pltpu.VMEM((nf, SUBLANE, tf), F32),
                        pltpu.VMEM((2, tm, tf), BF16)],
        compiler_params=_params(("arbitrary", "arbitrary")),
        name="conv_ffn_ln",
    )(x, mod_l, mod_l, mod_l, w_in, w_in, conv_w, conv_b.reshape(1, dff), w_down,
      ln_g.reshape(1, d), ln_b.reshape(1, d))


def _column_blocks(w, tn):
    k, n = w.shape
    return w.reshape(k, n // tn, tn).transpose(1, 0, 2).astype(BF16)


def _regroup_w_in(w, tn=512):
    o = np.cumsum([0, ATTN_W] + [KV_W] * 6 + [N_GATE * N_GROUP, LRU_W, LRU_W, GMLP_W, GMLP_W]).tolist()
    q, kc, vc, ks, vs, kw, vw, gl, zg, zr, zu, zv = [w[:, o[i]:o[i + 1]] for i in range(12)]
    pad = jnp.zeros((w.shape[0], ZB_COLS - ZB_GL - N_GATE * N_GROUP), w.dtype)
    wcat = jnp.concatenate([q, ks, vs, kw, vw, kc, vc, zg, zr, zu, zv, gl, pad], axis=1)
    return _column_blocks(wcat, tn), (ATTN_W + 4 * KV_W) // tn


def kernel(x, c, w_mod, b_mod, w_in, cmp_pos, cmp_w1, cmp_b1, cmp_w2, lru_conv_w, lru_conv_b, lru_wa,
           lru_ba, lru_wx, lru_bx, lru_lambda, sgu_ln_g, sgu_ln_b, sgu_w, sgu_b, w_o, ln1_g, ln1_b,
           ffn_w_in, ffn_conv_w, ffn_conv_b, ffn_w_down, ln2_g, ln2_b):
    batch, seq, d = x.shape
    depth = w_mod.shape[0]
    m = batch * seq
    x2 = x.reshape(m, d)
    c_pad = jnp.pad(c, ((0, SUBLANE - batch), (0, 0)))
    mod = _modulation(c_pad, w_mod, b_mod).reshape(depth, SUBLANE, 1, 6 * d)
    for l in range(depth):
        mod_l = mod[l]
        w_blocks, na = _regroup_w_in(w_in[l])
        za, zb = _in_proj(x2, mod_l, 1, 0, w_blocks, na, seq)
        cmp = _compress(zb, cmp_pos[l], cmp_w1[l], cmp_b1[l], cmp_w2[l], batch, seq)
        kc = cmp[0]
        vct = jnp.swapaxes(cmp[1], -1, -2)
        gl = zb[:, ZB_GL:ZB_GL + N_GATE * N_GROUP].reshape(m, N_GROUP, N_GATE).transpose(1, 0, 2)
        y_attn = _attention(za, kc, vct, gl, batch, seq)
        y_lru = _lru(zb, lru_conv_w[l], lru_conv_b[l], lru_wa[l], lru_ba[l], lru_wx[l], lru_bx[l],
                     lru_lambda[l], batch, seq)
        y_sgu = _sgu(zb, sgu_ln_g[l], sgu_ln_b[l], sgu_w[l], sgu_b[l], batch, seq)
        y = jnp.concatenate([y_attn, y_lru, y_sgu], axis=1)
        x2 = _oproj(y, w_o[l].astype(BF16), x2, mod_l, 2, ln1_g[l], ln1_b[l], seq)
        x2 = _ffn(x2, mod_l, 4, 3, 5, _column_blocks(ffn_w_in[l], 256), ffn_conv_w[l], ffn_conv_b[l],
                  ffn_w_down[l].astype(BF16), ln2_g[l], ln2_b[l], seq)
    return x2.reshape(batch, seq, d)
```

```python
import functools

import jax
import jax.numpy as jnp
import numpy as np
from jax import lax
from jax.experimental import pallas as pl
from jax.experimental.pallas import tpu as pltpu

F32 = jnp.float32
BF16 = jnp.bfloat16

DEPTH = 2
HEAD_DIM = 128
HPG = 4
N_GROUP = 4
ATTN_W = HPG * N_GROUP * HEAD_DIM
KV_W = N_GROUP * HEAD_DIM
LRU_W = 1024
LRU_BLOCKS = 8
LRU_CONV = 4
LRU_C = 8.0
GMLP_W = 1024
GMLP_GROUPS = 8
CHUNK = 128
CMP_LEN = 32
CMP_STRIDE = 16
SEL_BLOCK = 64
SEL_TOPK = 16
WINDOW = 512
FFN_CONV = 3
NEG = -1e30
FORCE = 1e4
LN_EPS = 1e-5
ALPHA = (2.0 * DEPTH) ** 0.25
SCALE = HEAD_DIM ** -0.5
EXP2_SCALE = SCALE * 1.4426950408889634
N_GATE = 3 * HPG
SEL_PAD = 128
CHAIN_ROWS = 512

V7X_VMEM_LIMIT_BYTES = 58 * 1024 * 1024
LANE = 128
SUBLANE = 8

ZB_KC, ZB_VC, ZB_G, ZB_R, ZB_U, ZB_V, ZB_GL = 0, 512, 1024, 2048, 3072, 4096, 5120
ZB_COLS = 5632


def _params(sem):
    return pltpu.CompilerParams(dimension_semantics=sem, vmem_limit_bytes=V7X_VMEM_LIMIT_BYTES)


def _gelu(x):
    return 0.5 * x * (1.0 + jnp.tanh(0.7978845608028654 * (x + 0.044715 * (x * x * x))))


def _layer_norm(v, g, b):
    mu = jnp.mean(v, axis=-1, keepdims=True)
    d = v - mu
    var = jnp.mean(d * d, axis=-1, keepdims=True)
    return d * lax.rsqrt(var + LN_EPS) * g + b


def _shift_rows(cur, prev_tail, k):
    r = pltpu.roll(cur, k, 0)
    rp = pltpu.roll(prev_tail, k, 0)
    row = lax.broadcasted_iota(jnp.int32, prev_tail.shape, 0)
    first = jnp.where(row < k, rp, r[0:SUBLANE])
    return jnp.concatenate([first, r[SUBLANE:]], axis=0)


def _modulate_into(h_ref, x_ref, sc_ref, sh_ref, row_chunk=128):
    scale = 1.0 + sc_ref[...]
    shift = sh_ref[...]

    def body(r, _):
        rs = pl.ds(pl.multiple_of(r * row_chunk, row_chunk), row_chunk)
        h_ref[rs, :] = (x_ref[rs, :] * scale + shift).astype(h_ref.dtype)
        return 0

    lax.fori_loop(0, h_ref.shape[0] // row_chunk, body, 0)


def _zero_rows(o_ref, row_chunk=128):
    def body(r, _):
        rs = pl.ds(pl.multiple_of(r * row_chunk, row_chunk), row_chunk)
        o_ref[rs, :] = jnp.zeros((row_chunk, o_ref.shape[1]), o_ref.dtype)
        return 0

    lax.fori_loop(0, o_ref.shape[0] // row_chunk, body, 0)


def _accumulate_matmul(o_ref, lhs, w_ref, col_chunk=1024):
    n = o_ref.shape[1]
    for c0 in range(0, n, col_chunk):
        cs = slice(c0, min(c0 + col_chunk, n))
        o_ref[:, cs] += jnp.dot(lhs, w_ref[:, cs], preferred_element_type=F32)


def _deepnorm_epilogue(o_ref, x_ref, gm_ref, lng_ref, lnb_ref, row_chunk=128):
    gate = 1.0 + gm_ref[...]
    lng = lng_ref[...]
    lnb = lnb_ref[...]

    def body(r, _):
        rs = pl.ds(pl.multiple_of(r * row_chunk, row_chunk), row_chunk)
        v = ALPHA * x_ref[rs, :] + gate * o_ref[rs, :]
        o_ref[rs, :] = _layer_norm(v, lng, lnb)
        return 0

    lax.fori_loop(0, o_ref.shape[0] // row_chunk, body, 0)


def _mod_kernel(c_ref, w_ref, b_ref, o_ref):
    c = c_ref[...]
    cs = (c * jax.nn.sigmoid(c)).astype(BF16)
    o_ref[...] = jnp.dot(cs, w_ref[...].astype(BF16), preferred_element_type=F32) + b_ref[...]


def _modulation(c_pad, w_mod, b_mod, tn=512):
    depth, d, n = w_mod.shape
    rows = c_pad.shape[0]
    return pl.pallas_call(
        _mod_kernel,
        grid=(depth, n // tn),
        in_specs=[
            pl.BlockSpec((rows, d), lambda l, j: (0, 0)),
            pl.BlockSpec((None, d, tn), lambda l, j: (l, 0, j)),
            pl.BlockSpec((None, 1, tn), lambda l, j: (l, 0, j)),
        ],
        out_specs=pl.BlockSpec((None, rows, tn), lambda l, j: (l, 0, j)),
        out_shape=jax.ShapeDtypeStruct((depth, rows, n), F32),
        compiler_params=_params(("arbitrary", "arbitrary")),
        name="modulation",
    )(c_pad, w_mod, b_mod.reshape(depth, 1, n))


def _in_proj_kernel(x_ref, sc_ref, sh_ref, w_ref, za_ref, zb_ref, h_ref, *, na):
    j = pl.program_id(1)

    @pl.when(j == 0)
    def _():
        _modulate_into(h_ref, x_ref, sc_ref, sh_ref)

    r = jnp.dot(h_ref[...], w_ref[...], preferred_element_type=F32)

    @pl.when(j < na)
    def _():
        za_ref[...] = r.astype(za_ref.dtype)

    @pl.when(j >= na)
    def _():
        zb_ref[...] = r.astype(zb_ref.dtype)


def _in_proj(x, mod_l, sc_idx, sh_idx, w_blocks, na, rows_per_batch, tm=1024):
    m, k = x.shape
    nb, _, tn = w_blocks.shape
    tpb = rows_per_batch // tm
    kern = functools.partial(_in_proj_kernel, na=na)
    return pl.pallas_call(
        kern,
        grid=(m // tm, nb),
        in_specs=[
            pl.BlockSpec((tm, k), lambda i, j: (i, 0)),
            pl.BlockSpec((None, 1, k), lambda i, j: (i // tpb, 0, sc_idx)),
            pl.BlockSpec((None, 1, k), lambda i, j: (i // tpb, 0, sh_idx)),
            pl.BlockSpec((None, k, tn), lambda i, j: (j, 0, 0)),
        ],
        out_specs=[
            pl.BlockSpec((tm, tn), lambda i, j: (i, jnp.minimum(j, na - 1))),
            pl.BlockSpec((tm, tn), lambda i, j: (i, jnp.maximum(j - na, 0))),
        ],
        out_shape=[jax.ShapeDtypeStruct((m, na * tn), BF16),
                   jax.ShapeDtypeStruct((m, (nb - na) * tn), F32)],
        scratch_shapes=[pltpu.VMEM((tm, k), BF16)],
        compiler_params=_params(("arbitrary", "arbitrary")),
        name="in_proj",
    )(x, mod_l, mod_l, w_blocks)


def _compress_kernel(kv_ref, pos_ref, w1_ref, b1_ref, w2_ref, o_ref, *, ncp):
    a0 = jnp.zeros((ncp, HEAD_DIM), F32)
    a1 = jnp.zeros((ncp, HEAD_DIM), F32)
    for r in range(CMP_STRIDE):
        xr = kv_ref[pl.ds(r, ncp, stride=CMP_STRIDE), :]
        lo = (xr + pos_ref[r:r + 1, :]).astype(BF16)
        hi = (xr + pos_ref[CMP_STRIDE + r:CMP_STRIDE + r + 1, :]).astype(BF16)
        a0 = a0 + jnp.dot(lo, w1_ref[r * HEAD_DIM:(r + 1) * HEAD_DIM, :], preferred_element_type=F32)
        a1 = a1 + jnp.dot(hi, w1_ref[(CMP_STRIDE + r) * HEAD_DIM:(CMP_STRIDE + r + 1) * HEAD_DIM, :],
                          preferred_element_type=F32)
    pre = a0 + pltpu.roll(a1, ncp - 1, 0) + b1_ref[...]
    hid = _gelu(pre).astype(BF16)
    o_ref[...] = jnp.dot(hid, w2_ref[...], preferred_element_type=F32).astype(o_ref.dtype)


def _compress(zb, cmp_pos, cmp_w1, cmp_b1, cmp_w2, batch, seq):
    ncp = seq // CMP_STRIDE
    kern = functools.partial(_compress_kernel, ncp=ncp)
    return pl.pallas_call(
        kern,
        grid=(2, batch, N_GROUP),
        in_specs=[
            pl.BlockSpec((seq, HEAD_DIM), lambda w, b, g: (b, w * N_GROUP + g)),
            pl.BlockSpec((None, CMP_LEN, HEAD_DIM), lambda w, b, g: (w, 0, 0)),
            pl.BlockSpec((None, CMP_LEN * HEAD_DIM, HEAD_DIM), lambda w, b, g: (w, 0, 0)),
            pl.BlockSpec((None, 1, HEAD_DIM), lambda w, b, g: (w, 0, 0)),
            pl.BlockSpec((None, HEAD_DIM, HEAD_DIM), lambda w, b, g: (w, 0, 0)),
        ],
        out_specs=pl.BlockSpec((None, None, None, ncp, HEAD_DIM), lambda w, b, g: (w, b, g, 0, 0)),
        out_shape=jax.ShapeDtypeStruct((2, batch, N_GROUP, ncp, HEAD_DIM), BF16),
        compiler_params=_params(("arbitrary", "arbitrary", "arbitrary")),
        name="nsa_compress",
    )(zb, cmp_pos, cmp_w1.astype(BF16), cmp_b1.reshape(2, 1, HEAD_DIM), cmp_w2.astype(BF16))


_NT = (((1,), (1,)), ((), ()))


def _attn_kernel(q_ref, ks_ref, vs_ref, kw_ref, vw_ref, ke_ref, kc_ref, vct_ref, cov_ref, gl_ref,
                 o_ref, qa_ref, *, tq, tk, ncp, nsel):
    half = min(HPG * tq, CHAIN_ROWS)
    nch = HPG * tq // half
    hpc = half // tq
    q0 = pl.program_id(2) * tq
    for h in range(HPG):
        qa_ref[h * tq:(h + 1) * tq, 0:HEAD_DIM] = q_ref[:, h * HEAD_DIM:(h + 1) * HEAD_DIM]

    ci = lax.broadcasted_iota(jnp.int32, (ncp, half), 0) * CMP_STRIDE + (CMP_LEN - 1)
    tc = lax.broadcasted_iota(jnp.int32, (ncp, half), 1) & (tq - 1)
    valid_c = ci - tc <= q0
    t_row = q0 + (lax.broadcasted_iota(jnp.int32, (1, half), 1) & (tq - 1))
    any_c = (t_row >= CMP_LEN - 1).astype(F32)
    o_ct, p_sum = [], None
    for c in range(nch):
        qc = qa_ref[c * half:(c + 1) * half, 0:HEAD_DIM]
        s_c = lax.dot_general(kc_ref[...], qc, _NT, preferred_element_type=F32)
        s_c = jnp.where(valid_c, s_c, NEG)
        m_c = jnp.max(s_c, axis=0, keepdims=True)
        p_c = jnp.exp2((s_c - m_c) * EXP2_SCALE)
        l_c = jnp.sum(p_c, axis=0, keepdims=True)
        p_c = p_c * (any_c / l_c)
        o_ct.append(jnp.dot(vct_ref[...], p_c.astype(BF16), preferred_element_type=F32))
        for h in range(hpc):
            ph = p_c[:, h * tq:(h + 1) * tq]
            p_sum = ph if p_sum is None else p_sum + ph

    a = q0 // tk
    ts = 2 * tk

    def rel(width):
        return ((lax.broadcasted_iota(jnp.int32, (half, width), 0) & (tq - 1))
                - lax.broadcasted_iota(jnp.int32, (half, width), 1))

    ones_col = jnp.where(lax.broadcasted_iota(jnp.int32, (ts, LANE), 1) == 0, 1.0, 0.0).astype(BF16)

    def with_ones(v):
        return jnp.concatenate([v, ones_col], axis=1)

    def online(carry, s, v_aug):
        m, acc = carry
        mn = jnp.maximum(m, jnp.max(s, axis=1, keepdims=True))
        al = jnp.exp2((m - mn) * EXP2_SCALE)
        p = jnp.exp2((s - mn) * EXP2_SCALE).astype(BF16)
        acc = al * acc + jnp.dot(p, v_aug, preferred_element_type=F32)
        return mn, acc

    def normalised(state, sl):
        acc = state[1]
        return acc[sl, 0:HEAD_DIM] * (1.0 / acc[sl, HEAD_DIM:HEAD_DIM + 1])

    init1 = (jnp.full((half, 1), NEG, F32), jnp.zeros((half, HEAD_DIM + LANE), F32))

    imp = jnp.dot(cov_ref[...], p_sum.astype(BF16), preferred_element_type=F32)
    jb = lax.broadcasted_iota(jnp.int32, (SEL_PAD, tq), 0)
    tl = q0 + lax.broadcasted_iota(jnp.int32, (SEL_PAD, tq), 1)
    cur = lax.shift_right_logical(tl, 6)
    forced = (jb == 0) | (jb == cur) | (jb == cur - 1)
    imp = jnp.where(forced, FORCE, jnp.where(jb * SEL_BLOCK <= tl, imp, NEG))
    if nsel < SEL_PAD:
        imp = jnp.where(jb < nsel, imp, -jnp.inf)
    selb = jnp.full((SEL_PAD, tq), NEG, F32)
    for _ in range(min(SEL_TOPK, nsel)):
        mx = jnp.max(imp, axis=0, keepdims=True)
        idx = jnp.min(jnp.where(imp == mx, jb, SEL_PAD), axis=0, keepdims=True)
        hit = jb == idx
        selb = jnp.where(hit, 0.0, selb)
        imp = jnp.where(hit, -jnp.inf, imp)
    selb_t = selb.T.astype(BF16)
    for h in range(HPG):
        qa_ref[h * tq:(h + 1) * tq, HEAD_DIM:HEAD_DIM + SEL_PAD] = selb_t

    def sel_tile(kt, carry, causal):
        start = pl.multiple_of(kt * ts, ts)
        kaug = jnp.concatenate([ks_ref[pl.ds(start, ts), :], ke_ref[pl.ds(start, ts), :]], axis=1)
        v = with_ones(vs_ref[pl.ds(start, ts), :])
        out = []
        for c in range(nch):
            s = lax.dot_general(qa_ref[c * half:(c + 1) * half, :], kaug, _NT, preferred_element_type=F32)
            if causal:
                s = jnp.where(rel(ts) >= start - q0, s, NEG)
            out.append(online(carry[c], s, v))
        return tuple(out)

    n_full = a // 2
    carry = lax.fori_loop(0, n_full, lambda kt, cr: sel_tile(kt, cr, False), (init1,) * nch)
    sel = sel_tile(n_full, carry, True)

    w0 = pl.multiple_of(jnp.maximum(a - 1, 0) * tk, tk)
    kw = kw_ref[pl.ds(w0, ts), :]
    vw = with_ones(vw_ref[pl.ds(w0, ts), :])
    rel_w = rel(ts)
    win = []
    for c in range(nch):
        s = lax.dot_general(qa_ref[c * half:(c + 1) * half, 0:HEAD_DIM], kw, _NT, preferred_element_type=F32)
        s = jnp.where(rel_w >= w0 - q0, jnp.where(rel_w < WINDOW + w0 - q0, s, NEG), NEG)
        win.append(online(init1, s, vw))

    gate = jax.nn.sigmoid(gl_ref[...])
    for h in range(HPG):
        c, hh = divmod(h, hpc)
        sl = slice(hh * tq, (hh + 1) * tq)
        o_c = o_ct[c][:, sl].T
        o_s = normalised(sel[c], sl)
        o_w = normalised(win[c], sl)
        out = (gate[:, 3 * h:3 * h + 1] * o_c + gate[:, 3 * h + 1:3 * h + 2] * o_s
               + gate[:, 3 * h + 2:3 * h + 3] * o_w)
        o_ref[:, h * HEAD_DIM:(h + 1) * HEAD_DIM] = out.astype(o_ref.dtype)


def _selection_constants(seq):
    ncp = seq // CMP_STRIDE
    nc = ncp - CMP_LEN // CMP_STRIDE + 1
    nsel = seq // SEL_BLOCK
    ci = np.arange(ncp)[None, :] * CMP_STRIDE
    sj = np.arange(SEL_PAD)[:, None] * SEL_BLOCK
    cover_t = ((ci < sj + SEL_BLOCK) & (ci + CMP_LEN > sj)
               & (np.arange(ncp)[None, :] < nc) & (np.arange(SEL_PAD)[:, None] < nsel))
    key_block = (np.arange(seq)[:, None] // SEL_BLOCK) == np.arange(SEL_PAD)[None, :]
    return jnp.asarray(cover_t, BF16), jnp.asarray(key_block, BF16)


def _attention(za, kc, vct, gl, batch, seq, tq=512, tk=512):
    assert tk == WINDOW and tk % tq == 0 and tq & (tq - 1) == 0 and seq % (2 * tk) == 0
    ncp = seq // CMP_STRIDE
    nsel = seq // SEL_BLOCK
    nq = seq // tq
    cover_t, key_block = _selection_constants(seq)
    qcols = ATTN_W // LANE
    kern = functools.partial(_attn_kernel, tq=tq, tk=tk, ncp=ncp, nsel=nsel)
    kv_spec = lambda off: pl.BlockSpec((seq, HEAD_DIM), lambda b, g, i: (b, qcols + off * N_GROUP + g))
    return pl.pallas_call(
        kern,
        grid=(batch, N_GROUP, nq),
        in_specs=[
            pl.BlockSpec((tq, HPG * HEAD_DIM), lambda b, g, i: (b * nq + i, g)),
            kv_spec(0), kv_spec(1), kv_spec(2), kv_spec(3),
            pl.BlockSpec((seq, SEL_PAD), lambda b, g, i: (0, 0)),
            pl.BlockSpec((None, None, ncp, HEAD_DIM), lambda b, g, i: (b, g, 0, 0)),
            pl.BlockSpec((None, None, HEAD_DIM, ncp), lambda b, g, i: (b, g, 0, 0)),
            pl.BlockSpec((SEL_PAD, ncp), lambda b, g, i: (0, 0)),
            pl.BlockSpec((None, tq, N_GATE), lambda b, g, i: (g, b * nq + i, 0)),
        ],
        out_specs=pl.BlockSpec((tq, HPG * HEAD_DIM), lambda b, g, i: (b * nq + i, g)),
        out_shape=jax.ShapeDtypeStruct((batch * seq, ATTN_W), BF16),
        scratch_shapes=[pltpu.VMEM((HPG * tq, HEAD_DIM + SEL_PAD), BF16)],
        compiler_params=_params(("arbitrary", "arbitrary", "arbitrary")),
        name="nsa_attention",
    )(za, za, za, za, za, key_block, kc, vct, cover_t, gl)


def _lru_kernel(zg_ref, zr_ref, cw_ref, cb_ref, wa_ref, ba_ref, wx_ref, bx_ref, lam_ref, o_ref,
                tail_ref, h_ref, *, tt):
    @pl.when(pl.program_id(1) == 0)
    def _():
        tail_ref[...] = jnp.zeros_like(tail_ref)
        h_ref[...] = jnp.zeros_like(h_ref)

    zr = zr_ref[...]
    width = zr.shape[1]
    prev = tail_ref[...]
    xr = cw_ref[LRU_CONV - 1:LRU_CONV, :] * zr + cb_ref[...]
    for k in range(1, LRU_CONV):
        xr = xr + cw_ref[LRU_CONV - 1 - k:LRU_CONV - k, :] * _shift_rows(zr, prev, k)
    tail_ref[...] = zr[tt - SUBLANE:tt]

    xb = xr.astype(BF16)
    bw = width // LRU_BLOCKS
    r_parts, i_parts = [], []
    for n in range(LRU_BLOCKS):
        xs = xb[:, n * bw:(n + 1) * bw]
        r_parts.append(jnp.dot(xs, wa_ref[n], preferred_element_type=F32))
        i_parts.append(jnp.dot(xs, wx_ref[n], preferred_element_type=F32))
    rg = jax.nn.sigmoid(jnp.concatenate(r_parts, axis=1) + ba_ref[...])
    ig = jax.nn.sigmoid(jnp.concatenate(i_parts, axis=1) + bx_ref[...])
    lam = lam_ref[...]
    softplus_neg = jnp.maximum(-lam, 0.0) + jnp.log1p(jnp.exp(-jnp.abs(lam)))
    log_a = (-LRU_C) * rg * softplus_neg
    a = jnp.exp(log_a)
    b = jnp.sqrt(1.0 - a * a) * (ig * xr)

    row = lax.broadcasted_iota(jnp.int32, (tt, width), 0)
    d = 1
    while d < tt:
        a_s = pltpu.roll(a, d, 0)
        b_s = pltpu.roll(b, d, 0)
        keep = row >= d
        b = jnp.where(keep, a * b_s + b, b)
        a = jnp.where(keep, a * a_s, a)
        d *= 2
    h = a * h_ref[...] + b
    h_ref[...] = h[tt - 1:tt]
    o_ref[...] = (_gelu(zg_ref[...]) * h).astype(o_ref.dtype)


def _lru(zb, conv_w, conv_b, wa, ba, wx, bx, lam, batch, seq, tt=256):
    nt = seq // tt
    cg, cr = ZB_G // LRU_W, ZB_R // LRU_W
    kern = functools.partial(_lru_kernel, tt=tt)
    vec = lambda: pl.BlockSpec((1, LRU_W), lambda b, t: (0, 0))
    mat = lambda: pl.BlockSpec((LRU_BLOCKS, LRU_W // LRU_BLOCKS, LRU_W // LRU_BLOCKS), lambda b, t: (0, 0, 0))
    return pl.pallas_call(
        kern,
        grid=(batch, nt),
        in_specs=[
            pl.BlockSpec((tt, LRU_W), lambda b, t: (b * nt + t, cg)),
            pl.BlockSpec((tt, LRU_W), lambda b, t: (b * nt + t, cr)),
            pl.BlockSpec((LRU_CONV, LRU_W), lambda b, t: (0, 0)),
            vec(), mat(), vec(), mat(), vec(), vec(),
        ],
        out_specs=pl.BlockSpec((tt, LRU_W), lambda b, t: (b * nt + t, 0)),
        out_shape=jax.ShapeDtypeStruct((batch * seq, LRU_W), BF16),
        scratch_shapes=[pltpu.VMEM((SUBLANE, LRU_W), F32), pltpu.VMEM((1, LRU_W), F32)],
        compiler_params=_params(("arbitrary", "arbitrary")),
        name="rg_lru",
    )(zb, zb, conv_w, conv_b.reshape(1, LRU_W), wa.astype(BF16), ba.reshape(1, LRU_W),
      wx.astype(BF16), bx.reshape(1, LRU_W), lam.reshape(1, LRU_W))


def _sgu_kernel(zu_ref, zv_ref, g_ref, b_ref, ws_ref, bst_ref, o_ref, *, nchunk):
    v = _layer_norm(_gelu(zv_ref[...]), g_ref[...], b_ref[...]).astype(BF16)
    u = _gelu(zu_ref[...])
    gw = GMLP_W // GMLP_GROUPS
    tril = (lax.broadcasted_iota(jnp.int32, (CHUNK, CHUNK), 0)
            >= lax.broadcasted_iota(jnp.int32, (CHUNK, CHUNK), 1))
    for gi in range(GMLP_GROUPS):
        cs = slice(gi * gw, (gi + 1) * gw)
        w = jnp.where(tril, ws_ref[gi], 0.0).astype(BF16)
        rhs = jnp.concatenate([v[c * CHUNK:(c + 1) * CHUNK, cs] for c in range(nchunk)], axis=1)
        y = jnp.dot(w, rhs, preferred_element_type=F32) + bst_ref[:, gi:gi + 1]
        for c in range(nchunk):
            rs = slice(c * CHUNK, (c + 1) * CHUNK)
            o_ref[rs, cs] = (u[rs, cs] * y[:, c * gw:(c + 1) * gw]).astype(o_ref.dtype)


def _sgu(zb, ln_g, ln_b, w_s, b_s, batch, seq, nchunk=4):
    tt = nchunk * CHUNK
    nt = batch * seq // tt
    cu, cv = ZB_U // GMLP_W, ZB_V // GMLP_W
    kern = functools.partial(_sgu_kernel, nchunk=nchunk)
    return pl.pallas_call(
        kern,
        grid=(nt,),
        in_specs=[
            pl.BlockSpec((tt, GMLP_W), lambda t: (t, cu)),
            pl.BlockSpec((tt, GMLP_W), lambda t: (t, cv)),
            pl.BlockSpec((1, GMLP_W), lambda t: (0, 0)),
            pl.BlockSpec((1, GMLP_W), lambda t: (0, 0)),
            pl.BlockSpec((GMLP_GROUPS, CHUNK, CHUNK), lambda t: (0, 0, 0)),
            pl.BlockSpec((CHUNK, GMLP_GROUPS), lambda t: (0, 0)),
        ],
        out_specs=pl.BlockSpec((tt, GMLP_W), lambda t: (t, 0)),
        out_shape=jax.ShapeDtypeStruct((batch * seq, GMLP_W), BF16),
        compiler_params=_params(("arbitrary",)),
        name="spatial_gating",
    )(zb, zb, ln_g.reshape(1, GMLP_W), ln_b.reshape(1, GMLP_W), w_s, b_s.T)


def _oproj_kernel(ya_ref, yl_ref, ys_ref, w_ref, x_ref, gm_ref, lng_ref, lnb_ref, o_ref, *, na):
    k = pl.program_id(1)

    @pl.when(k == 0)
    def _():
        _zero_rows(o_ref)

    lhs = jnp.where(k < na, ya_ref[...], jnp.where(k == na, yl_ref[...], ys_ref[...]))
    _accumulate_matmul(o_ref, lhs, w_ref)

    @pl.when(k == pl.num_programs(1) - 1)
    def _():
        _deepnorm_epilogue(o_ref, x_ref, gm_ref, lng_ref, lnb_ref)


def _oproj(y_attn, y_lru, y_sgu, w, x, mod_l, gate_idx, ln_g, ln_b, rows_per_batch, tm=512):
    m = y_attn.shape[0]
    tk = y_lru.shape[1]
    assert y_sgu.shape[1] == tk and y_attn.shape[1] % tk == 0
    na = y_attn.shape[1] // tk
    kdim, d = w.shape
    assert kdim == (na + 2) * tk
    tpb = rows_per_batch // tm
    return pl.pallas_call(
        functools.partial(_oproj_kernel, na=na),
        grid=(m // tm, na + 2),
        in_specs=[
            pl.BlockSpec((tm, tk), lambda i, k: (i, jnp.minimum(k, na - 1))),
            pl.BlockSpec((tm, tk), lambda i, k: (i, 0)),
            pl.BlockSpec((tm, tk), lambda i, k: (i, 0)),
            pl.BlockSpec((tk, d), lambda i, k: (k, 0)),
            pl.BlockSpec((tm, d), lambda i, k: (i, 0), pipeline_mode=pl.Buffered(1)),
            pl.BlockSpec((None, 1, d), lambda i, k: (i // tpb, 0, gate_idx)),
            pl.BlockSpec((1, d), lambda i, k: (0, 0)),
            pl.BlockSpec((1, d), lambda i, k: (0, 0)),
        ],
        out_specs=pl.BlockSpec((tm, d), lambda i, k: (i, 0)),
        out_shape=jax.ShapeDtypeStruct((m, d), F32),
        compiler_params=_params(("arbitrary", "arbitrary")),
        name="out_proj_ln",
    )(y_attn, y_lru, y_sgu, w, x, mod_l, ln_g.reshape(1, d), ln_b.reshape(1, d))


def _ffn_kernel(x_ref, sc_ref, sh_ref, gm_ref, wg_ref, wu_ref, cw_ref, cb_ref, wd_ref, lng_ref, lnb_ref,
                o_ref, h_ref, tail_ref, act_ref, *, tm, tpb, nf):
    i = pl.program_id(0)
    f = pl.program_id(1)

    @pl.when(f == 0)
    def _():
        _modulate_into(h_ref, x_ref, sc_ref, sh_ref)
        _zero_rows(o_ref)
        act_ref[1] = jnp.zeros(act_ref.shape[1:], act_ref.dtype)

    @pl.when((f == 0) & (i == 0))
    def _():
        tail_ref[...] = jnp.zeros(tail_ref.shape, tail_ref.dtype)

    fb = jnp.minimum(f, nf - 1)
    h = h_ref[...]
    g = jnp.dot(h, wg_ref[...], preferred_element_type=F32)
    u = jnp.dot(h, wu_ref[...], preferred_element_type=F32)
    _accumulate_matmul(o_ref, act_ref[(f + 1) % 2], wd_ref)

    prev = jnp.where(i % tpb == 0, 0.0, tail_ref[fb])
    tail_ref[fb] = g[tm - SUBLANE:tm]
    gc = cw_ref[FFN_CONV - 1:FFN_CONV, :] * g + cb_ref[...]
    for k in range(1, FFN_CONV):
        gc = gc + cw_ref[FFN_CONV - 1 - k:FFN_CONV - k, :] * _shift_rows(g, prev, k)
    act_ref[f % 2] = (gc * jax.nn.sigmoid(gc) * u).astype(BF16)

    @pl.when(f == nf)
    def _():
        _deepnorm_epilogue(o_ref, x_ref, gm_ref, lng_ref, lnb_ref)


def _ffn(x, mod_l, sc_idx, sh_idx, gate_idx, w_in, conv_w, conv_b, w_down, ln_g, ln_b, rows_per_batch,
         tm=512, tf=256):
    m, d = x.shape
    dff = w_down.shape[0]
    nf = dff // tf
    tpb = rows_per_batch // tm
    kern = functools.partial(_ffn_kernel, tm=tm, tpb=tpb, nf=nf)
    cur = lambda f: jnp.minimum(f, nf - 1)
    prv = lambda f: jnp.maximum(f - 1, 0)
    return pl.pallas_call(
        kern,
        grid=(m // tm, nf + 1),
        in_specs=[
            pl.BlockSpec((tm, d), lambda i, f: (i, 0), pipeline_mode=pl.Buffered(1)),
            pl.BlockSpec((None, 1, d), lambda i, f: (i // tpb, 0, sc_idx)),
            pl.BlockSpec((None, 1, d), lambda i, f: (i // tpb, 0, sh_idx)),
            pl.BlockSpec((None, 1, d), lambda i, f: (i // tpb, 0, gate_idx)),
            pl.BlockSpec((d, tf), lambda i, f: (0, cur(f))),
            pl.BlockSpec((d, tf), lambda i, f: (0, nf + cur(f))),
            pl.BlockSpec((FFN_CONV, tf), lambda i, f: (0, cur(f))),
            pl.BlockSpec((1, tf), lambda i, f: (0, cur(f))),
            pl.BlockSpec((tf, d), lambda i, f: (prv(f), 0)),
            pl.BlockSpec((1, d), lambda i, f: (0, 0)),
            pl.BlockSpec((1, d), lambda i, f: (0, 0)),
        ],
        out_specs=pl.BlockSpec((tm, d), lambda i, f: (i, 0)),
        out_shape=jax.ShapeDtypeStruct((m, d), F32),
        scratch_shapes=[pltpu.VMEM((tm, d), BF16), pltpu.VMEM((nf, SUBLANE, tf), F32),
                        pltpu.VMEM((2, tm, tf), BF16)],
        compiler_params=_params(("arbitrary", "arbitrary")),
        name="conv_ffn_ln",
    )(x, mod_l, mod_l, mod_l, w_in, w_in, conv_w, conv_b.reshape(1, dff), w_down,
      ln_g.reshape(1, d), ln_b.reshape(1, d))


def _column_blocks(w, tn):
    k, n = w.shape
    return w.reshape(k, n // tn, tn).transpose(1, 0, 2).astype(BF16)


def _regroup_w_in(w, tn=512):
    o = np.cumsum([0, ATTN_W] + [KV_W] * 6 + [N_GATE * N_GROUP, LRU_W, LRU_W, GMLP_W, GMLP_W]).tolist()
    q, kc, vc, ks, vs, kw, vw, gl, zg, zr, zu, zv = [w[:, o[i]:o[i + 1]] for i in range(12)]
    pad = jnp.zeros((w.shape[0], ZB_COLS - ZB_GL - N_GATE * N_GROUP), w.dtype)
    wcat = jnp.concatenate([q, ks, vs, kw, vw, kc, vc, zg, zr, zu, zv, gl, pad], axis=1)
    return _column_blocks(wcat, tn), (ATTN_W + 4 * KV_W) // tn


def kernel(x, c, w_mod, b_mod, w_in, cmp_pos, cmp_w1, cmp_b1, cmp_w2, lru_conv_w, lru_conv_b, lru_wa,
           lru_ba, lru_wx, lru_bx, lru_lambda, sgu_ln_g, sgu_ln_b, sgu_w, sgu_b, w_o, ln1_g, ln1_b,
           ffn_w_in, ffn_conv_w, ffn_conv_b, ffn_w_down, ln2_g, ln2_b):
    batch, seq, d = x.shape
    depth = w_mod.shape[0]
    m = batch * seq
    x2 = x.reshape(m, d)
    c_pad = jnp.pad(c, ((0, SUBLANE - batch), (0, 0)))
    mod = _modulation(c_pad, w_mod, b_mod).reshape(depth, SUBLANE, 1, 6 * d)
    for l in range(depth):
        mod_l = mod[l]
        w_blocks, na = _regroup_w_in(w_in[l])
        za, zb = _in_proj(x2, mod_l, 1, 0, w_blocks, na, seq)
        cmp = _compress(zb, cmp_pos[l], cmp_w1[l], cmp_b1[l], cmp_w2[l], batch, seq)
        kc = cmp[0]
        vct = jnp.swapaxes(cmp[1], -1, -2)
        gl = zb[:, ZB_GL:ZB_GL + N_GATE * N_GROUP].reshape(m, N_GROUP, N_GATE).transpose(1, 0, 2)
        y_attn = _attention(za, kc, vct, gl, batch, seq)
        y_lru = _lru(zb, lru_conv_w[l], lru_conv_b[l], lru_wa[l], lru_ba[l], lru_wx[l], lru_bx[l],
                     lru_lambda[l], batch, seq)
        y_sgu = _sgu(zb, sgu_ln_g[l], sgu_ln_b[l], sgu_w[l], sgu_b[l], batch, seq)
        x2 = _oproj(y_attn, y_lru, y_sgu, w_o[l].astype(BF16), x2, mod_l, 2, ln1_g[l], ln1_b[l], seq)
        x2 = _ffn(x2, mod_l, 4, 3, 5, ffn_w_in[l].astype(BF16), ffn_conv_w[l], ffn_conv_b[l],
                  ffn_w_down[l].astype(BF16), ln2_g[l], ln2_b[l], seq)
    return x2.reshape(batch, seq, d)
```

```python
import functools

import jax
import jax.numpy as jnp
import numpy as np
from jax import lax
from jax.experimental import pallas as pl
from jax.experimental.pallas import tpu as pltpu

F32 = jnp.float32
BF16 = jnp.bfloat16

DEPTH = 2
HEAD_DIM = 128
HPG = 4
N_GROUP = 4
ATTN_W = HPG * N_GROUP * HEAD_DIM
KV_W = N_GROUP * HEAD_DIM
LRU_W = 1024
LRU_BLOCKS = 8
LRU_CONV = 4
LRU_C = 8.0
GMLP_W = 1024
GMLP_GROUPS = 8
CHUNK = 128
CMP_LEN = 32
CMP_STRIDE = 16
SEL_BLOCK = 64
SEL_TOPK = 16
WINDOW = 512
FFN_CONV = 3
NEG = -1e30
FORCE = 1e4
LN_EPS = 1e-5
ALPHA = (2.0 * DEPTH) ** 0.25
SCALE = HEAD_DIM ** -0.5
EXP2_SCALE = SCALE * 1.4426950408889634
N_GATE = 3 * HPG
SEL_PAD = 128
CHAIN_ROWS = 512

V7X_VMEM_LIMIT_BYTES = 58 * 1024 * 1024
LANE = 128
SUBLANE = 8

ZB_KC, ZB_VC, ZB_G, ZB_R, ZB_U, ZB_V, ZB_GL = 0, 512, 1024, 2048, 3072, 4096, 5120
ZB_COLS = 5632


def _params(sem):
    return pltpu.CompilerParams(dimension_semantics=sem, vmem_limit_bytes=V7X_VMEM_LIMIT_BYTES)


def _gelu(x):
    return 0.5 * x * (1.0 + jnp.tanh(0.7978845608028654 * (x + 0.044715 * (x * x * x))))


def _layer_norm(v, g, b):
    mu = jnp.mean(v, axis=-1, keepdims=True)
    d = v - mu
    var = jnp.mean(d * d, axis=-1, keepdims=True)
    return d * lax.rsqrt(var + LN_EPS) * g + b


def _shift_rows(cur, prev_tail, k):
    r = pltpu.roll(cur, k, 0)
    rp = pltpu.roll(prev_tail, k, 0)
    row = lax.broadcasted_iota(jnp.int32, prev_tail.shape, 0)
    first = jnp.where(row < k, rp, r[0:SUBLANE])
    return jnp.concatenate([first, r[SUBLANE:]], axis=0)


def _modulate_into(h_ref, x_ref, sc_ref, sh_ref, row_chunk=128):
    scale = 1.0 + sc_ref[...]
    shift = sh_ref[...]

    def body(r, _):
        rs = pl.ds(pl.multiple_of(r * row_chunk, row_chunk), row_chunk)
        h_ref[rs, :] = (x_ref[rs, :] * scale + shift).astype(h_ref.dtype)
        return 0

    lax.fori_loop(0, h_ref.shape[0] // row_chunk, body, 0)


def _zero_rows(o_ref, row_chunk=128):
    def body(r, _):
        rs = pl.ds(pl.multiple_of(r * row_chunk, row_chunk), row_chunk)
        o_ref[rs, :] = jnp.zeros((row_chunk, o_ref.shape[1]), o_ref.dtype)
        return 0

    lax.fori_loop(0, o_ref.shape[0] // row_chunk, body, 0)


def _accumulate_matmul(o_ref, lhs, w_ref, col_chunk=1024):
    n = o_ref.shape[1]
    for c0 in range(0, n, col_chunk):
        cs = slice(c0, min(c0 + col_chunk, n))
        o_ref[:, cs] += jnp.dot(lhs, w_ref[:, cs], preferred_element_type=F32)


def _deepnorm_epilogue(o_ref, x_ref, gm_ref, lng_ref, lnb_ref, row_chunk=128):
    gate = 1.0 + gm_ref[...]
    lng = lng_ref[...]
    lnb = lnb_ref[...]

    def body(r, _):
        rs = pl.ds(pl.multiple_of(r * row_chunk, row_chunk), row_chunk)
        v = ALPHA * x_ref[rs, :] + gate * o_ref[rs, :]
        o_ref[rs, :] = _layer_norm(v, lng, lnb)
        return 0

    lax.fori_loop(0, o_ref.shape[0] // row_chunk, body, 0)


def _mod_kernel(c_ref, w_ref, b_ref, o_ref):
    c = c_ref[...]
    cs = (c * jax.nn.sigmoid(c)).astype(BF16)
    o_ref[...] = jnp.dot(cs, w_ref[...].astype(BF16), preferred_element_type=F32) + b_ref[...]


def _modulation(c_pad, w_mod, b_mod, tn=512):
    depth, d, n = w_mod.shape
    rows = c_pad.shape[0]
    return pl.pallas_call(
        _mod_kernel,
        grid=(depth, n // tn),
        in_specs=[
            pl.BlockSpec((rows, d), lambda l, j: (0, 0)),
            pl.BlockSpec((None, d, tn), lambda l, j: (l, 0, j)),
            pl.BlockSpec((None, 1, tn), lambda l, j: (l, 0, j)),
        ],
        out_specs=pl.BlockSpec((None, rows, tn), lambda l, j: (l, 0, j)),
        out_shape=jax.ShapeDtypeStruct((depth, rows, n), F32),
        compiler_params=_params(("arbitrary", "arbitrary")),
        name="modulation",
    )(c_pad, w_mod, b_mod.reshape(depth, 1, n))


def _in_proj_kernel(x_ref, sc_ref, sh_ref, w_ref, za_ref, zb_ref, h_ref, *, na):
    j = pl.program_id(1)

    @pl.when(j == 0)
    def _():
        _modulate_into(h_ref, x_ref, sc_ref, sh_ref)

    r = jnp.dot(h_ref[...], w_ref[...], preferred_element_type=F32)

    @pl.when(j < na)
    def _():
        za_ref[...] = r.astype(za_ref.dtype)

    @pl.when(j >= na)
    def _():
        zb_ref[...] = r.astype(zb_ref.dtype)


def _block_order_map(order):
    runs, start = [], 0
    for j in range(1, len(order) + 1):
        if j == len(order) or order[j] != order[j - 1] + 1:
            runs.append((j, order[start] - start))
            start = j

    def block_of(j):
        out = j + runs[-1][1]
        for end, off in reversed(runs[:-1]):
            out = jnp.where(j < end, j + off, out)
        return out

    return block_of


def _in_proj(x, mod_l, sc_idx, sh_idx, w, order, na, rows_per_batch, tm=1024, tn=512):
    m, k = x.shape
    nb = len(order)
    block_of = _block_order_map(order)
    tpb = rows_per_batch // tm
    kern = functools.partial(_in_proj_kernel, na=na)
    return pl.pallas_call(
        kern,
        grid=(m // tm, nb),
        in_specs=[
            pl.BlockSpec((tm, k), lambda i, j: (i, 0)),
            pl.BlockSpec((None, 1, k), lambda i, j: (i // tpb, 0, sc_idx)),
            pl.BlockSpec((None, 1, k), lambda i, j: (i // tpb, 0, sh_idx)),
            pl.BlockSpec((k, tn), lambda i, j: (0, block_of(j))),
        ],
        out_specs=[
            pl.BlockSpec((tm, tn), lambda i, j: (i, jnp.minimum(j, na - 1))),
            pl.BlockSpec((tm, tn), lambda i, j: (i, jnp.maximum(j - na, 0))),
        ],
        out_shape=[jax.ShapeDtypeStruct((m, na * tn), BF16),
                   jax.ShapeDtypeStruct((m, (nb - na) * tn), F32)],
        scratch_shapes=[pltpu.VMEM((tm, k), BF16)],
        compiler_params=_params(("arbitrary", "arbitrary")),
        name="in_proj",
    )(x, mod_l, mod_l, w)


def _compress_kernel(kv_ref, pos_ref, w1_ref, b1_ref, w2_ref, o_ref, *, ncp):
    a0 = jnp.zeros((ncp, HEAD_DIM), F32)
    a1 = jnp.zeros((ncp, HEAD_DIM), F32)
    for r in range(CMP_STRIDE):
        xr = kv_ref[pl.ds(r, ncp, stride=CMP_STRIDE), :]
        lo = (xr + pos_ref[r:r + 1, :]).astype(BF16)
        hi = (xr + pos_ref[CMP_STRIDE + r:CMP_STRIDE + r + 1, :]).astype(BF16)
        a0 = a0 + jnp.dot(lo, w1_ref[r * HEAD_DIM:(r + 1) * HEAD_DIM, :], preferred_element_type=F32)
        a1 = a1 + jnp.dot(hi, w1_ref[(CMP_STRIDE + r) * HEAD_DIM:(CMP_STRIDE + r + 1) * HEAD_DIM, :],
                          preferred_element_type=F32)
    pre = a0 + pltpu.roll(a1, ncp - 1, 0) + b1_ref[...]
    hid = _gelu(pre).astype(BF16)
    o_ref[...] = jnp.dot(hid, w2_ref[...], preferred_element_type=F32).astype(o_ref.dtype)


def _compress(zb, cmp_pos, cmp_w1, cmp_b1, cmp_w2, batch, seq):
    ncp = seq // CMP_STRIDE
    kern = functools.partial(_compress_kernel, ncp=ncp)
    return pl.pallas_call(
        kern,
        grid=(2, batch, N_GROUP),
        in_specs=[
            pl.BlockSpec((seq, HEAD_DIM), lambda w, b, g: (b, w * N_GROUP + g)),
            pl.BlockSpec((None, CMP_LEN, HEAD_DIM), lambda w, b, g: (w, 0, 0)),
            pl.BlockSpec((None, CMP_LEN * HEAD_DIM, HEAD_DIM), lambda w, b, g: (w, 0, 0)),
            pl.BlockSpec((None, 1, HEAD_DIM), lambda w, b, g: (w, 0, 0)),
            pl.BlockSpec((None, HEAD_DIM, HEAD_DIM), lambda w, b, g: (w, 0, 0)),
        ],
        out_specs=pl.BlockSpec((None, None, None, ncp, HEAD_DIM), lambda w, b, g: (w, b, g, 0, 0)),
        out_shape=jax.ShapeDtypeStruct((2, batch, N_GROUP, ncp, HEAD_DIM), BF16),
        compiler_params=_params(("arbitrary", "arbitrary", "arbitrary")),
        name="nsa_compress",
    )(zb, cmp_pos, cmp_w1.astype(BF16), cmp_b1.reshape(2, 1, HEAD_DIM), cmp_w2.astype(BF16))


_NT = (((1,), (1,)), ((), ()))


def _attn_kernel(q_ref, ks_ref, vs_ref, kw_ref, vw_ref, ke_ref, kc_ref, vct_ref, cov_ref, gl_ref,
                 o_ref, qa_ref, *, tq, tk, ncp, nsel):
    half = min(HPG * tq, CHAIN_ROWS)
    nch = HPG * tq // half
    hpc = half // tq
    q0 = pl.program_id(2) * tq
    for h in range(HPG):
        qa_ref[h * tq:(h + 1) * tq, 0:HEAD_DIM] = q_ref[:, h * HEAD_DIM:(h + 1) * HEAD_DIM]

    ci = lax.broadcasted_iota(jnp.int32, (ncp, half), 0) * CMP_STRIDE + (CMP_LEN - 1)
    tc = lax.broadcasted_iota(jnp.int32, (ncp, half), 1) & (tq - 1)
    valid_c = ci - tc <= q0
    t_row = q0 + (lax.broadcasted_iota(jnp.int32, (1, half), 1) & (tq - 1))
    any_c = (t_row >= CMP_LEN - 1).astype(F32)
    o_ct, p_sum = [], None
    for c in range(nch):
        qc = qa_ref[c * half:(c + 1) * half, 0:HEAD_DIM]
        s_c = lax.dot_general(kc_ref[...], qc, _NT, preferred_element_type=F32)
        s_c = jnp.where(valid_c, s_c, NEG)
        m_c = jnp.max(s_c, axis=0, keepdims=True)
        p_c = jnp.exp2((s_c - m_c) * EXP2_SCALE)
        l_c = jnp.sum(p_c, axis=0, keepdims=True)
        p_c = p_c * (any_c / l_c)
        o_ct.append(jnp.dot(vct_ref[...], p_c.astype(BF16), preferred_element_type=F32))
        for h in range(hpc):
            ph = p_c[:, h * tq:(h + 1) * tq]
            p_sum = ph if p_sum is None else p_sum + ph

    a = q0 // tk
    ts = 2 * tk

    def rel(width):
        return ((lax.broadcasted_iota(jnp.int32, (half, width), 0) & (tq - 1))
                - lax.broadcasted_iota(jnp.int32, (half, width), 1))

    ones_col = jnp.where(lax.broadcasted_iota(jnp.int32, (ts, LANE), 1) == 0, 1.0, 0.0).astype(BF16)

    def with_ones(v):
        return jnp.concatenate([v, ones_col], axis=1)

    def online(carry, s, v_aug):
        m, acc = carry
        mn = jnp.maximum(m, jnp.max(s, axis=1, keepdims=True))
        al = jnp.exp2((m - mn) * EXP2_SCALE)
        p = jnp.exp2((s - mn) * EXP2_SCALE).astype(BF16)
        acc = al * acc + jnp.dot(p, v_aug, preferred_element_type=F32)
        return mn, acc

    def normalised(state, sl):
        acc = state[1]
        return acc[sl, 0:HEAD_DIM] * (1.0 / acc[sl, HEAD_DIM:HEAD_DIM + 1])

    init1 = (jnp.full((half, 1), NEG, F32), jnp.zeros((half, HEAD_DIM + LANE), F32))

    imp = jnp.dot(cov_ref[...], p_sum.astype(BF16), preferred_element_type=F32)
    jb = lax.broadcasted_iota(jnp.int32, (SEL_PAD, tq), 0)
    tl = q0 + lax.broadcasted_iota(jnp.int32, (SEL_PAD, tq), 1)
    cur = lax.shift_right_logical(tl, 6)
    forced = (jb == 0) | (jb == cur) | (jb == cur - 1)
    imp = jnp.where(forced, FORCE, jnp.where(jb * SEL_BLOCK <= tl, imp, NEG))
    if nsel < SEL_PAD:
        imp = jnp.where(jb < nsel, imp, -jnp.inf)
    selb = jnp.full((SEL_PAD, tq), NEG, F32)
    for _ in range(min(SEL_TOPK, nsel)):
        mx = jnp.max(imp, axis=0, keepdims=True)
        idx = jnp.min(jnp.where(imp == mx, jb, SEL_PAD), axis=0, keepdims=True)
        hit = jb == idx
        selb = jnp.where(hit, 0.0, selb)
        imp = jnp.where(hit, -jnp.inf, imp)
    selb_t = selb.T.astype(BF16)
    for h in range(HPG):
        qa_ref[h * tq:(h + 1) * tq, HEAD_DIM:HEAD_DIM + SEL_PAD] = selb_t

    def sel_tile(kt, carry, causal):
        start = pl.multiple_of(kt * ts, ts)
        kaug = jnp.concatenate([ks_ref[pl.ds(start, ts), :], ke_ref[pl.ds(start, ts), :]], axis=1)
        v = with_ones(vs_ref[pl.ds(start, ts), :])
        out = []
        for c in range(nch):
            s = lax.dot_general(qa_ref[c * half:(c + 1) * half, :], kaug, _NT, preferred_element_type=F32)
            if causal:
                s = jnp.where(rel(ts) >= start - q0, s, NEG)
            out.append(online(carry[c], s, v))
        return tuple(out)

    n_full = a // 2
    carry = lax.fori_loop(0, n_full, lambda kt, cr: sel_tile(kt, cr, False), (init1,) * nch)
    sel = sel_tile(n_full, carry, True)

    w0 = pl.multiple_of(jnp.maximum(a - 1, 0) * tk, tk)
    kw = kw_ref[pl.ds(w0, ts), :]
    vw = with_ones(vw_ref[pl.ds(w0, ts), :])
    rel_w = rel(ts)
    win = []
    for c in range(nch):
        s = lax.dot_general(qa_ref[c * half:(c + 1) * half, 0:HEAD_DIM], kw, _NT, preferred_element_type=F32)
        s = jnp.where(rel_w >= w0 - q0, jnp.where(rel_w < WINDOW + w0 - q0, s, NEG), NEG)
        win.append(online(init1, s, vw))

    gate = jax.nn.sigmoid(gl_ref[...])
    for h in range(HPG):
        c, hh = divmod(h, hpc)
        sl = slice(hh * tq, (hh + 1) * tq)
        o_c = o_ct[c][:, sl].T
        o_s = normalised(sel[c], sl)
        o_w = normalised(win[c], sl)
        out = (gate[:, 3 * h:3 * h + 1] * o_c + gate[:, 3 * h + 1:3 * h + 2] * o_s
               + gate[:, 3 * h + 2:3 * h + 3] * o_w)
        o_ref[:, h * HEAD_DIM:(h + 1) * HEAD_DIM] = out.astype(o_ref.dtype)


def _selection_constants(seq):
    ncp = seq // CMP_STRIDE
    nc = ncp - CMP_LEN // CMP_STRIDE + 1
    nsel = seq // SEL_BLOCK
    ci = np.arange(ncp)[None, :] * CMP_STRIDE
    sj = np.arange(SEL_PAD)[:, None] * SEL_BLOCK
    cover_t = ((ci < sj + SEL_BLOCK) & (ci + CMP_LEN > sj)
               & (np.arange(ncp)[None, :] < nc) & (np.arange(SEL_PAD)[:, None] < nsel))
    key_block = (np.arange(seq)[:, None] // SEL_BLOCK) == np.arange(SEL_PAD)[None, :]
    return jnp.asarray(cover_t, BF16), jnp.asarray(key_block, BF16)


def _attention(za, kc, vct, gl, batch, seq, tq=512, tk=512):
    assert tk == WINDOW and tk % tq == 0 and tq & (tq - 1) == 0 and seq % (2 * tk) == 0
    ncp = seq // CMP_STRIDE
    nsel = seq // SEL_BLOCK
    nq = seq // tq
    cover_t, key_block = _selection_constants(seq)
    qcols = ATTN_W // LANE
    kern = functools.partial(_attn_kernel, tq=tq, tk=tk, ncp=ncp, nsel=nsel)
    kv_spec = lambda off: pl.BlockSpec((seq, HEAD_DIM), lambda b, g, i: (b, qcols + off * N_GROUP + g))
    return pl.pallas_call(
        kern,
        grid=(batch, N_GROUP, nq),
        in_specs=[
            pl.BlockSpec((tq, HPG * HEAD_DIM), lambda b, g, i: (b * nq + i, g)),
            kv_spec(0), kv_spec(1), kv_spec(2), kv_spec(3),
            pl.BlockSpec((seq, SEL_PAD), lambda b, g, i: (0, 0)),
            pl.BlockSpec((None, None, ncp, HEAD_DIM), lambda b, g, i: (b, g, 0, 0)),
            pl.BlockSpec((None, None, HEAD_DIM, ncp), lambda b, g, i: (b, g, 0, 0)),
            pl.BlockSpec((SEL_PAD, ncp), lambda b, g, i: (0, 0)),
            pl.BlockSpec((None, tq, N_GATE), lambda b, g, i: (g, b * nq + i, 0)),
        ],
        out_specs=pl.BlockSpec((tq, HPG * HEAD_DIM), lambda b, g, i: (b * nq + i, g)),
        out_shape=jax.ShapeDtypeStruct((batch * seq, ATTN_W), BF16),
        scratch_shapes=[pltpu.VMEM((HPG * tq, HEAD_DIM + SEL_PAD), BF16)],
        compiler_params=_params(("arbitrary", "arbitrary", "arbitrary")),
        name="nsa_attention",
    )(za, za, za, za, za, key_block, kc, vct, cover_t, gl)


def _lru_kernel(zg_ref, zr_ref, cw_ref, cb_ref, wa_ref, ba_ref, wx_ref, bx_ref, lam_ref, o_ref,
                tail_ref, h_ref, *, tt):
    @pl.when(pl.program_id(1) == 0)
    def _():
        tail_ref[...] = jnp.zeros_like(tail_ref)
        h_ref[...] = jnp.zeros_like(h_ref)

    zr = zr_ref[...]
    width = zr.shape[1]
    prev = tail_ref[...]
    xr = cw_ref[LRU_CONV - 1:LRU_CONV, :] * zr + cb_ref[...]
    for k in range(1, LRU_CONV):
        xr = xr + cw_ref[LRU_CONV - 1 - k:LRU_CONV - k, :] * _shift_rows(zr, prev, k)
    tail_ref[...] = zr[tt - SUBLANE:tt]

    xb = xr.astype(BF16)
    bw = width // LRU_BLOCKS
    r_parts, i_parts = [], []
    for n in range(LRU_BLOCKS):
        xs = xb[:, n * bw:(n + 1) * bw]
        r_parts.append(jnp.dot(xs, wa_ref[n], preferred_element_type=F32))
        i_parts.append(jnp.dot(xs, wx_ref[n], preferred_element_type=F32))
    rg = jax.nn.sigmoid(jnp.concatenate(r_parts, axis=1) + ba_ref[...])
    ig = jax.nn.sigmoid(jnp.concatenate(i_parts, axis=1) + bx_ref[...])
    lam = lam_ref[...]
    softplus_neg = jnp.maximum(-lam, 0.0) + jnp.log1p(jnp.exp(-jnp.abs(lam)))
    log_a = (-LRU_C) * rg * softplus_neg
    a = jnp.exp(log_a)
    b = jnp.sqrt(1.0 - a * a) * (ig * xr)

    row_in_group = lax.broadcasted_iota(jnp.int32, (tt, width), 0) & (SUBLANE - 1)
    d = 1
    while d < SUBLANE:
        a_s = pltpu.roll(a, d, 0)
        b_s = pltpu.roll(b, d, 0)
        keep = row_in_group >= d
        b = jnp.where(keep, a * b_s + b, b)
        a = jnp.where(keep, a * a_s, a)
        d *= 2
    h_prev = h_ref[...]
    groups = []
    for j in range(tt // SUBLANE):
        rs = slice(j * SUBLANE, (j + 1) * SUBLANE)
        h_j = a[rs] * h_prev + b[rs]
        groups.append(h_j)
        h_prev = h_j[SUBLANE - 1:SUBLANE]
    h = jnp.concatenate(groups, axis=0)
    h_ref[...] = h_prev
    o_ref[...] = (_gelu(zg_ref[...]) * h).astype(o_ref.dtype)


def _lru(zb, conv_w, conv_b, wa, ba, wx, bx, lam, batch, seq, tt=256):
    nt = seq // tt
    cg, cr = ZB_G // LRU_W, ZB_R // LRU_W
    kern = functools.partial(_lru_kernel, tt=tt)
    vec = lambda: pl.BlockSpec((1, LRU_W), lambda b, t: (0, 0))
    mat = lambda: pl.BlockSpec((LRU_BLOCKS, LRU_W // LRU_BLOCKS, LRU_W // LRU_BLOCKS), lambda b, t: (0, 0, 0))
    return pl.pallas_call(
        kern,
        grid=(batch, nt),
        in_specs=[
            pl.BlockSpec((tt, LRU_W), lambda b, t: (b * nt + t, cg)),
            pl.BlockSpec((tt, LRU_W), lambda b, t: (b * nt + t, cr)),
            pl.BlockSpec((LRU_CONV, LRU_W), lambda b, t: (0, 0)),
            vec(), mat(), vec(), mat(), vec(), vec(),
        ],
        out_specs=pl.BlockSpec((tt, LRU_W), lambda b, t: (b * nt + t, 0)),
        out_shape=jax.ShapeDtypeStruct((batch * seq, LRU_W), BF16),
        scratch_shapes=[pltpu.VMEM((SUBLANE, LRU_W), F32), pltpu.VMEM((1, LRU_W), F32)],
        compiler_params=_params(("arbitrary", "arbitrary")),
        name="rg_lru",
    )(zb, zb, conv_w, conv_b.reshape(1, LRU_W), wa.astype(BF16), ba.reshape(1, LRU_W),
      wx.astype(BF16), bx.reshape(1, LRU_W), lam.reshape(1, LRU_W))


def _sgu_kernel(zu_ref, zv_ref, g_ref, b_ref, ws_ref, bst_ref, o_ref, *, nchunk):
    v = _layer_norm(_gelu(zv_ref[...]), g_ref[...], b_ref[...]).astype(BF16)
    u = _gelu(zu_ref[...])
    gw = GMLP_W // GMLP_GROUPS
    tril = (lax.broadcasted_iota(jnp.int32, (CHUNK, CHUNK), 0)
            >= lax.broadcasted_iota(jnp.int32, (CHUNK, CHUNK), 1))
    for gi in range(GMLP_GROUPS):
        cs = slice(gi * gw, (gi + 1) * gw)
        w = jnp.where(tril, ws_ref[gi], 0.0).astype(BF16)
        rhs = jnp.concatenate([v[c * CHUNK:(c + 1) * CHUNK, cs] for c in range(nchunk)], axis=1)
        y = jnp.dot(w, rhs, preferred_element_type=F32) + bst_ref[:, gi:gi + 1]
        for c in range(nchunk):
            rs = slice(c * CHUNK, (c + 1) * CHUNK)
            o_ref[rs, cs] = (u[rs, cs] * y[:, c * gw:(c + 1) * gw]).astype(o_ref.dtype)


def _sgu(zb, ln_g, ln_b, w_s, b_s, batch, seq, nchunk=4):
    tt = nchunk * CHUNK
    nt = batch * seq // tt
    cu, cv = ZB_U // GMLP_W, ZB_V // GMLP_W
    kern = functools.partial(_sgu_kernel, nchunk=nchunk)
    return pl.pallas_call(
        kern,
        grid=(nt,),
        in_specs=[
            pl.BlockSpec((tt, GMLP_W), lambda t: (t, cu)),
            pl.BlockSpec((tt, GMLP_W), lambda t: (t, cv)),
            pl.BlockSpec((1, GMLP_W), lambda t: (0, 0)),
            pl.BlockSpec((1, GMLP_W), lambda t: (0, 0)),
            pl.BlockSpec((GMLP_GROUPS, CHUNK, CHUNK), lambda t: (0, 0, 0)),
            pl.BlockSpec((CHUNK, GMLP_GROUPS), lambda t: (0, 0)),
        ],
        out_specs=pl.BlockSpec((tt, GMLP_W), lambda t: (t, 0)),
        out_shape=jax.ShapeDtypeStruct((batch * seq, GMLP_W), BF16),
        compiler_params=_params(("arbitrary",)),
        name="spatial_gating",
    )(zb, zb, ln_g.reshape(1, GMLP_W), ln_b.reshape(1, GMLP_W), w_s, b_s.T)


def _oproj_kernel(ya_ref, yl_ref, ys_ref, w_ref, x_ref, gm_ref, lng_ref, lnb_ref, o_ref, *, na):
    k = pl.program_id(1)

    @pl.when(k == 0)
    def _():
        _zero_rows(o_ref)

    lhs = jnp.where(k < na, ya_ref[...], jnp.where(k == na, yl_ref[...], ys_ref[...]))
    _accumulate_matmul(o_ref, lhs, w_ref)

    @pl.when(k == pl.num_programs(1) - 1)
    def _():
        _deepnorm_epilogue(o_ref, x_ref, gm_ref, lng_ref, lnb_ref)


def _oproj(y_attn, y_lru, y_sgu, w, x, mod_l, gate_idx, ln_g, ln_b, rows_per_batch, tm=512):
    m = y_attn.shape[0]
    tk = y_lru.shape[1]
    assert y_sgu.shape[1] == tk and y_attn.shape[1] % tk == 0
    na = y_attn.shape[1] // tk
    kdim, d = w.shape
    assert kdim == (na + 2) * tk
    tpb = rows_per_batch // tm
    return pl.pallas_call(
        functools.partial(_oproj_kernel, na=na),
        grid=(m // tm, na + 2),
        in_specs=[
            pl.BlockSpec((tm, tk), lambda i, k: (i, jnp.minimum(k, na - 1))),
            pl.BlockSpec((tm, tk), lambda i, k: (i, 0)),
            pl.BlockSpec((tm, tk), lambda i, k: (i, 0)),
            pl.BlockSpec((tk, d), lambda i, k: (k, 0)),
            pl.BlockSpec((tm, d), lambda i, k: (i, 0), pipeline_mode=pl.Buffered(1)),
            pl.BlockSpec((None, 1, d), lambda i, k: (i // tpb, 0, gate_idx)),
            pl.BlockSpec((1, d), lambda i, k: (0, 0)),
            pl.BlockSpec((1, d), lambda i, k: (0, 0)),
        ],
        out_specs=pl.BlockSpec((tm, d), lambda i, k: (i, 0)),
        out_shape=jax.ShapeDtypeStruct((m, d), F32),
        compiler_params=_params(("arbitrary", "arbitrary")),
        name="out_proj_ln",
    )(y_attn, y_lru, y_sgu, w, x, mod_l, ln_g.reshape(1, d), ln_b.reshape(1, d))


def _ffn_kernel(x_ref, sc_ref, sh_ref, gm_ref, wg_ref, wu_ref, cw_ref, cb_ref, wd_ref, wd_last_ref,
                lng_ref, lnb_ref, o_ref, h_ref, tail_ref, act_ref, *, tm, tpb, nf):
    i = pl.program_id(0)
    f = pl.program_id(1)

    @pl.when(f == 0)
    def _():
        _modulate_into(h_ref, x_ref, sc_ref, sh_ref)
        _zero_rows(o_ref)
        act_ref[1] = jnp.zeros(act_ref.shape[1:], act_ref.dtype)

    @pl.when((f == 0) & (i == 0))
    def _():
        tail_ref[...] = jnp.zeros(tail_ref.shape, tail_ref.dtype)

    h = h_ref[...]
    g = jnp.dot(h, wg_ref[...], preferred_element_type=F32)
    u = jnp.dot(h, wu_ref[...], preferred_element_type=F32)
    _accumulate_matmul(o_ref, act_ref[(f + 1) % 2], wd_ref)

    prev = jnp.where(i % tpb == 0, 0.0, tail_ref[f])
    tail_ref[f] = g[tm - SUBLANE:tm]
    gc = cw_ref[FFN_CONV - 1:FFN_CONV, :] * g + cb_ref[...]
    for k in range(1, FFN_CONV):
        gc = gc + cw_ref[FFN_CONV - 1 - k:FFN_CONV - k, :] * _shift_rows(g, prev, k)
    act_ref[f % 2] = (gc * jax.nn.sigmoid(gc) * u).astype(BF16)

    @pl.when(f == nf - 1)
    def _():
        _accumulate_matmul(o_ref, act_ref[(nf - 1) % 2], wd_last_ref)
        _deepnorm_epilogue(o_ref, x_ref, gm_ref, lng_ref, lnb_ref)


def _ffn(x, mod_l, sc_idx, sh_idx, gate_idx, w_in, conv_w, conv_b, w_down, ln_g, ln_b, rows_per_batch,
         tm=512, tf=256):
    m, d = x.shape
    dff = w_down.shape[0]
    nf = dff // tf
    tpb = rows_per_batch // tm
    kern = functools.partial(_ffn_kernel, tm=tm, tpb=tpb, nf=nf)
    prv = lambda f: jnp.maximum(f - 1, 0)
    return pl.pallas_call(
        kern,
        grid=(m // tm, nf),
        in_specs=[
            pl.BlockSpec((tm, d), lambda i, f: (i, 0), pipeline_mode=pl.Buffered(1)),
            pl.BlockSpec((None, 1, d), lambda i, f: (i // tpb, 0, sc_idx)),
            pl.BlockSpec((None, 1, d), lambda i, f: (i // tpb, 0, sh_idx)),
            pl.BlockSpec((None, 1, d), lambda i, f: (i // tpb, 0, gate_idx)),
            pl.BlockSpec((d, tf), lambda i, f: (0, f)),
            pl.BlockSpec((d, tf), lambda i, f: (0, nf + f)),
            pl.BlockSpec((FFN_CONV, tf), lambda i, f: (0, f)),
            pl.BlockSpec((1, tf), lambda i, f: (0, f)),
            pl.BlockSpec((tf, d), lambda i, f: (prv(f), 0)),
            pl.BlockSpec((tf, d), lambda i, f: (nf - 1, 0), pipeline_mode=pl.Buffered(1)),
            pl.BlockSpec((1, d), lambda i, f: (0, 0)),
            pl.BlockSpec((1, d), lambda i, f: (0, 0)),
        ],
        out_specs=pl.BlockSpec((tm, d), lambda i, f: (i, 0)),
        out_shape=jax.ShapeDtypeStruct((m, d), F32),
        scratch_shapes=[pltpu.VMEM((tm, d), BF16), pltpu.VMEM((nf, SUBLANE, tf), F32),
                        pltpu.VMEM((2, tm, tf), BF16)],
        compiler_params=_params(("arbitrary", "arbitrary")),
        name="conv_ffn_ln",
    )(x, mod_l, mod_l, mod_l, w_in, w_in, conv_w, conv_b.reshape(1, dff), w_down, w_down,
      ln_g.reshape(1, d), ln_b.reshape(1, d))


def _prepare_w_in(w, tn=512):
    a0 = ATTN_W + 6 * KV_W
    a1 = a0 + N_GATE * N_GROUP
    pad = jnp.zeros((w.shape[0], ZB_COLS - ZB_GL - N_GATE * N_GROUP), w.dtype)
    wn = jnp.concatenate([w[:, :a0], w[:, a1:], w[:, a0:a1], pad], axis=1).astype(BF16)
    blk = lambda lo, hi: tuple(range(lo // tn, hi // tn))
    qb = blk(0, ATTN_W)
    kcvc = blk(ATTN_W, ATTN_W + 2 * KV_W)
    kv4 = blk(ATTN_W + 2 * KV_W, a0)
    rest = blk(a0, wn.shape[1])
    return wn, qb + kv4 + kcvc + rest, len(qb + kv4)


def kernel(x, c, w_mod, b_mod, w_in, cmp_pos, cmp_w1, cmp_b1, cmp_w2, lru_conv_w, lru_conv_b, lru_wa,
           lru_ba, lru_wx, lru_bx, lru_lambda, sgu_ln_g, sgu_ln_b, sgu_w, sgu_b, w_o, ln1_g, ln1_b,
           ffn_w_in, ffn_conv_w, ffn_conv_b, ffn_w_down, ln2_g, ln2_b):
    batch, seq, d = x.shape
    depth = w_mod.shape[0]
    m = batch * seq
    x2 = x.reshape(m, d)
    c_pad = jnp.pad(c, ((0, SUBLANE - batch), (0, 0)))
    mod = _modulation(c_pad, w_mod, b_mod).reshape(depth, SUBLANE, 1, 6 * d)
    for l in range(depth):
        mod_l = mod[l]
        wn, order, na = _prepare_w_in(w_in[l])
        za, zb = _in_proj(x2, mod_l, 1, 0, wn, order, na, seq)
        cmp = _compress(zb, cmp_pos[l], cmp_w1[l], cmp_b1[l], cmp_w2[l], batch, seq)
        kc = cmp[0]
        vct = jnp.swapaxes(cmp[1], -1, -2)
        gl = zb[:, ZB_GL:ZB_GL + N_GATE * N_GROUP].reshape(m, N_GROUP, N_GATE).transpose(1, 0, 2)
        y_attn = _attention(za, kc, vct, gl, batch, seq)
        y_lru = _lru(zb, lru_conv_w[l], lru_conv_b[l], lru_wa[l], lru_ba[l], lru_wx[l], lru_bx[l],
                     lru_lambda[l], batch, seq)
        y_sgu = _sgu(zb, sgu_ln_g[l], sgu_ln_b[l], sgu_w[l], sgu_b[l], batch, seq)
        x2 = _oproj(y_attn, y_lru, y_sgu, w_o[l].astype(BF16), x2, mod_l, 2, ln1_g[l], ln1_b[l], seq)
        x2 = _ffn(x2, mod_l, 4, 3, 5, ffn_w_in[l].astype(BF16), ffn_conv_w[l], ffn_conv_b[l],
                  ffn_w_down[l].astype(BF16), ln2_g[l], ln2_b[l], seq)
    return x2.reshape(batch, seq, d)
```

```python
import functools

import jax
import jax.numpy as jnp
import numpy as np
from jax import lax
from jax.experimental import pallas as pl
from jax.experimental.pallas import tpu as pltpu

F32 = jnp.float32
BF16 = jnp.bfloat16

DEPTH = 2
HEAD_DIM = 128
HPG = 4
N_GROUP = 4
ATTN_W = HPG * N_GROUP * HEAD_DIM
KV_W = N_GROUP * HEAD_DIM
LRU_W = 1024
LRU_BLOCKS = 8
LRU_CONV = 4
LRU_C = 8.0
GMLP_W = 1024
GMLP_GROUPS = 8
CHUNK = 128
CMP_LEN = 32
CMP_STRIDE = 16
SEL_BLOCK = 64
SEL_TOPK = 16
WINDOW = 512
FFN_CONV = 3
NEG = -1e30
FORCE = 1e4
LN_EPS = 1e-5
ALPHA = (2.0 * DEPTH) ** 0.25
SCALE = HEAD_DIM ** -0.5
EXP2_SCALE = SCALE * 1.4426950408889634
N_GATE = 3 * HPG
SEL_PAD = 128
CHAIN_ROWS = 512

V7X_VMEM_LIMIT_BYTES = 58 * 1024 * 1024
LANE = 128
SUBLANE = 8

ZB_KC, ZB_VC, ZB_G, ZB_R, ZB_U, ZB_V, ZB_GL = 0, 512, 1024, 2048, 3072, 4096, 5120
ZB_COLS = 5632


def _params(sem):
    return pltpu.CompilerParams(dimension_semantics=sem, vmem_limit_bytes=V7X_VMEM_LIMIT_BYTES)


def _gelu(x):
    return 0.5 * x * (1.0 + jnp.tanh(0.7978845608028654 * (x + 0.044715 * (x * x * x))))


def _layer_norm(v, g, b):
    mu = jnp.mean(v, axis=-1, keepdims=True)
    d = v - mu
    var = jnp.mean(d * d, axis=-1, keepdims=True)
    return d * lax.rsqrt(var + LN_EPS) * g + b


def _shift_rows(cur, prev_tail, k):
    r = pltpu.roll(cur, k, 0)
    rp = pltpu.roll(prev_tail, k, 0)
    row = lax.broadcasted_iota(jnp.int32, prev_tail.shape, 0)
    first = jnp.where(row < k, rp, r[0:SUBLANE])
    return jnp.concatenate([first, r[SUBLANE:]], axis=0)


def _modulate_into(h_ref, x_ref, sc_ref, sh_ref, row_chunk=128):
    scale = 1.0 + sc_ref[...]
    shift = sh_ref[...]

    def body(r, _):
        rs = pl.ds(pl.multiple_of(r * row_chunk, row_chunk), row_chunk)
        h_ref[rs, :] = (x_ref[rs, :] * scale + shift).astype(h_ref.dtype)
        return 0

    lax.fori_loop(0, h_ref.shape[0] // row_chunk, body, 0)


def _zero_rows(o_ref, row_chunk=128):
    def body(r, _):
        rs = pl.ds(pl.multiple_of(r * row_chunk, row_chunk), row_chunk)
        o_ref[rs, :] = jnp.zeros((row_chunk, o_ref.shape[1]), o_ref.dtype)
        return 0

    lax.fori_loop(0, o_ref.shape[0] // row_chunk, body, 0)


def _accumulate_matmul(o_ref, lhs, w_ref, col_chunk=1024):
    n = o_ref.shape[1]
    for c0 in range(0, n, col_chunk):
        cs = slice(c0, min(c0 + col_chunk, n))
        o_ref[:, cs] += jnp.dot(lhs, w_ref[:, cs], preferred_element_type=F32)


def _deepnorm_epilogue(o_ref, x_ref, gm_ref, lng_ref, lnb_ref, row_chunk=128):
    gate = 1.0 + gm_ref[...]
    lng = lng_ref[...]
    lnb = lnb_ref[...]

    def body(r, _):
        rs = pl.ds(pl.multiple_of(r * row_chunk, row_chunk), row_chunk)
        v = ALPHA * x_ref[rs, :] + gate * o_ref[rs, :]
        o_ref[rs, :] = _layer_norm(v, lng, lnb)
        return 0

    lax.fori_loop(0, o_ref.shape[0] // row_chunk, body, 0)


def _mod_kernel(c_ref, w_ref, b_ref, o_ref):
    c = c_ref[...]
    cs = (c * jax.nn.sigmoid(c)).astype(BF16)
    o_ref[...] = jnp.dot(cs, w_ref[...].astype(BF16), preferred_element_type=F32) + b_ref[...]


def _modulation(c_pad, w_mod, b_mod, tn=1024):
    depth, d, n = w_mod.shape
    rows = c_pad.shape[0]
    return pl.pallas_call(
        _mod_kernel,
        grid=(depth, n // tn),
        in_specs=[
            pl.BlockSpec((rows, d), lambda l, j: (0, 0)),
            pl.BlockSpec((None, d, tn), lambda l, j: (l, 0, j)),
            pl.BlockSpec((None, 1, tn), lambda l, j: (l, 0, j)),
        ],
        out_specs=pl.BlockSpec((None, rows, tn), lambda l, j: (l, 0, j)),
        out_shape=jax.ShapeDtypeStruct((depth, rows, n), F32),
        compiler_params=_params(("arbitrary", "arbitrary")),
        name="modulation",
    )(c_pad, w_mod, b_mod.reshape(depth, 1, n))


def _in_proj_kernel(x_ref, sc_ref, sh_ref, w_ref, za_ref, zb_ref, h_ref, *, na):
    j = pl.program_id(1)

    @pl.when(j == 0)
    def _():
        _modulate_into(h_ref, x_ref, sc_ref, sh_ref)

    r = jnp.dot(h_ref[...], w_ref[...], preferred_element_type=F32)

    @pl.when(j < na)
    def _():
        za_ref[...] = r.astype(za_ref.dtype)

    @pl.when(j >= na)
    def _():
        zb_ref[...] = r.astype(zb_ref.dtype)


def _block_order_map(order):
    runs, start = [], 0
    for j in range(1, len(order) + 1):
        if j == len(order) or order[j] != order[j - 1] + 1:
            runs.append((j, order[start] - start))
            start = j

    def block_of(j):
        out = j + runs[-1][1]
        for end, off in reversed(runs[:-1]):
            out = jnp.where(j < end, j + off, out)
        return out

    return block_of


def _in_proj(x, mod_l, sc_idx, sh_idx, w, order, na, rows_per_batch, tm=1024, tn=512):
    m, k = x.shape
    nb = len(order)
    block_of = _block_order_map(order)
    tpb = rows_per_batch // tm
    kern = functools.partial(_in_proj_kernel, na=na)
    return pl.pallas_call(
        kern,
        grid=(m // tm, nb),
        in_specs=[
            pl.BlockSpec((tm, k), lambda i, j: (i, 0)),
            pl.BlockSpec((None, 1, k), lambda i, j: (i // tpb, 0, sc_idx)),
            pl.BlockSpec((None, 1, k), lambda i, j: (i // tpb, 0, sh_idx)),
            pl.BlockSpec((k, tn), lambda i, j: (0, block_of(j))),
        ],
        out_specs=[
            pl.BlockSpec((tm, tn), lambda i, j: (i, jnp.minimum(j, na - 1))),
            pl.BlockSpec((tm, tn), lambda i, j: (i, jnp.maximum(j - na, 0))),
        ],
        out_shape=[jax.ShapeDtypeStruct((m, na * tn), BF16),
                   jax.ShapeDtypeStruct((m, (nb - na) * tn), F32)],
        scratch_shapes=[pltpu.VMEM((tm, k), BF16)],
        compiler_params=_params(("arbitrary", "arbitrary")),
        name="in_proj",
    )(x, mod_l, mod_l, w)


def _compress_kernel(kv_ref, pos_ref, w1_ref, b1_ref, w2_ref, o_ref, *, ncp):
    a0 = jnp.zeros((ncp, HEAD_DIM), F32)
    a1 = jnp.zeros((ncp, HEAD_DIM), F32)
    for r in range(CMP_STRIDE):
        xr = kv_ref[pl.ds(r, ncp, stride=CMP_STRIDE), :]
        lo = (xr + pos_ref[r:r + 1, :]).astype(BF16)
        hi = (xr + pos_ref[CMP_STRIDE + r:CMP_STRIDE + r + 1, :]).astype(BF16)
        a0 = a0 + jnp.dot(lo, w1_ref[r * HEAD_DIM:(r + 1) * HEAD_DIM, :], preferred_element_type=F32)
        a1 = a1 + jnp.dot(hi, w1_ref[(CMP_STRIDE + r) * HEAD_DIM:(CMP_STRIDE + r + 1) * HEAD_DIM, :],
                          preferred_element_type=F32)
    pre = a0 + pltpu.roll(a1, ncp - 1, 0) + b1_ref[...]
    hid = _gelu(pre).astype(BF16)
    o_ref[...] = jnp.dot(hid, w2_ref[...], preferred_element_type=F32).astype(o_ref.dtype)


def _compress(zb, cmp_pos, cmp_w1, cmp_b1, cmp_w2, batch, seq):
    ncp = seq // CMP_STRIDE
    kern = functools.partial(_compress_kernel, ncp=ncp)
    return pl.pallas_call(
        kern,
        grid=(2, batch, N_GROUP),
        in_specs=[
            pl.BlockSpec((seq, HEAD_DIM), lambda w, b, g: (b, w * N_GROUP + g)),
            pl.BlockSpec((None, CMP_LEN, HEAD_DIM), lambda w, b, g: (w, 0, 0)),
            pl.BlockSpec((None, CMP_LEN * HEAD_DIM, HEAD_DIM), lambda w, b, g: (w, 0, 0)),
            pl.BlockSpec((None, 1, HEAD_DIM), lambda w, b, g: (w, 0, 0)),
            pl.BlockSpec((None, HEAD_DIM, HEAD_DIM), lambda w, b, g: (w, 0, 0)),
        ],
        out_specs=pl.BlockSpec((None, None, None, ncp, HEAD_DIM), lambda w, b, g: (w, b, g, 0, 0)),
        out_shape=jax.ShapeDtypeStruct((2, batch, N_GROUP, ncp, HEAD_DIM), BF16),
        compiler_params=_params(("arbitrary", "arbitrary", "arbitrary")),
        name="nsa_compress",
    )(zb, cmp_pos, cmp_w1.astype(BF16), cmp_b1.reshape(2, 1, HEAD_DIM), cmp_w2.astype(BF16))


_NT = (((1,), (1,)), ((), ()))


def _attn_kernel(q_ref, ks_ref, vs_ref, kw_ref, vw_ref, ke_ref, kc_ref, vct_ref, cov_ref, gl_ref,
                 o_ref, qa_ref, *, tq, tk, ncp, nsel):
    half = min(HPG * tq, CHAIN_ROWS)
    nch = HPG * tq // half
    hpc = half // tq
    q0 = pl.program_id(2) * tq
    for h in range(HPG):
        qa_ref[h * tq:(h + 1) * tq, 0:HEAD_DIM] = q_ref[:, h * HEAD_DIM:(h + 1) * HEAD_DIM]

    ci = lax.broadcasted_iota(jnp.int32, (ncp, half), 0) * CMP_STRIDE + (CMP_LEN - 1)
    tc = lax.broadcasted_iota(jnp.int32, (ncp, half), 1) & (tq - 1)
    valid_c = ci - tc <= q0
    t_row = q0 + (lax.broadcasted_iota(jnp.int32, (1, half), 1) & (tq - 1))
    any_c = (t_row >= CMP_LEN - 1).astype(F32)
    o_ct, p_sum = [], None
    for c in range(nch):
        qc = qa_ref[c * half:(c + 1) * half, 0:HEAD_DIM]
        s_c = lax.dot_general(kc_ref[...], qc, _NT, preferred_element_type=F32)
        s_c = jnp.where(valid_c, s_c, NEG)
        m_c = jnp.max(s_c, axis=0, keepdims=True)
        p_c = jnp.exp2((s_c - m_c) * EXP2_SCALE)
        l_c = jnp.sum(p_c, axis=0, keepdims=True)
        p_c = p_c * (any_c / l_c)
        o_ct.append(jnp.dot(vct_ref[...], p_c.astype(BF16), preferred_element_type=F32))
        for h in range(hpc):
            ph = p_c[:, h * tq:(h + 1) * tq]
            p_sum = ph if p_sum is None else p_sum + ph

    a = q0 // tk
    ts = 2 * tk

    def rel(width):
        return ((lax.broadcasted_iota(jnp.int32, (half, width), 0) & (tq - 1))
                - lax.broadcasted_iota(jnp.int32, (half, width), 1))

    ones_col = jnp.where(lax.broadcasted_iota(jnp.int32, (ts, LANE), 1) == 0, 1.0, 0.0).astype(BF16)

    def with_ones(v):
        return jnp.concatenate([v, ones_col], axis=1)

    def online(carry, s, v_aug):
        m, acc = carry
        mn = jnp.maximum(m, jnp.max(s, axis=1, keepdims=True))
        al = jnp.exp2((m - mn) * EXP2_SCALE)
        p = jnp.exp2((s - mn) * EXP2_SCALE).astype(BF16)
        acc = al * acc + jnp.dot(p, v_aug, preferred_element_type=F32)
        return mn, acc

    def normalised(state, sl):
        acc = state[1]
        return acc[sl, 0:HEAD_DIM] * (1.0 / acc[sl, HEAD_DIM:HEAD_DIM + 1])

    init1 = (jnp.full((half, 1), NEG, F32), jnp.zeros((half, HEAD_DIM + LANE), F32))

    imp = jnp.dot(cov_ref[...], p_sum.astype(BF16), preferred_element_type=F32)
    jb = lax.broadcasted_iota(jnp.int32, (SEL_PAD, tq), 0)
    tl = q0 + lax.broadcasted_iota(jnp.int32, (SEL_PAD, tq), 1)
    cur = lax.shift_right_logical(tl, 6)
    forced = (jb == 0) | (jb == cur) | (jb == cur - 1)
    imp = jnp.where(forced, FORCE, jnp.where(jb * SEL_BLOCK <= tl, imp, NEG))
    if nsel < SEL_PAD:
        imp = jnp.where(jb < nsel, imp, -jnp.inf)
    selb = jnp.full((SEL_PAD, tq), NEG, F32)
    for _ in range(min(SEL_TOPK, nsel)):
        mx = jnp.max(imp, axis=0, keepdims=True)
        idx = jnp.min(jnp.where(imp == mx, jb, SEL_PAD), axis=0, keepdims=True)
        hit = jb == idx
        selb = jnp.where(hit, 0.0, selb)
        imp = jnp.where(hit, -jnp.inf, imp)
    selb_t = selb.T.astype(BF16)
    for h in range(HPG):
        qa_ref[h * tq:(h + 1) * tq, HEAD_DIM:HEAD_DIM + SEL_PAD] = selb_t

    def sel_tile(kt, carry, causal):
        start = pl.multiple_of(kt * ts, ts)
        kaug = jnp.concatenate([ks_ref[pl.ds(start, ts), :], ke_ref[pl.ds(start, ts), :]], axis=1)
        v = with_ones(vs_ref[pl.ds(start, ts), :])
        out = []
        for c in range(nch):
            s = lax.dot_general(qa_ref[c * half:(c + 1) * half, :], kaug, _NT, preferred_element_type=F32)
            if causal:
                s = jnp.where(rel(ts) >= start - q0, s, NEG)
            out.append(online(carry[c], s, v))
        return tuple(out)

    n_full = a // 2
    carry = lax.fori_loop(0, n_full, lambda kt, cr: sel_tile(kt, cr, False), (init1,) * nch)
    sel = sel_tile(n_full, carry, True)

    w0 = pl.multiple_of(jnp.maximum(a - 1, 0) * tk, tk)
    kw = kw_ref[pl.ds(w0, ts), :]
    vw = with_ones(vw_ref[pl.ds(w0, ts), :])
    rel_w = rel(ts)
    win = []
    for c in range(nch):
        s = lax.dot_general(qa_ref[c * half:(c + 1) * half, 0:HEAD_DIM], kw, _NT, preferred_element_type=F32)
        s = jnp.where(rel_w >= w0 - q0, jnp.where(rel_w < WINDOW + w0 - q0, s, NEG), NEG)
        win.append(online(init1, s, vw))

    gate = jax.nn.sigmoid(gl_ref[...])
    for h in range(HPG):
        c, hh = divmod(h, hpc)
        sl = slice(hh * tq, (hh + 1) * tq)
        o_c = o_ct[c][:, sl].T
        o_s = normalised(sel[c], sl)
        o_w = normalised(win[c], sl)
        out = (gate[:, 3 * h:3 * h + 1] * o_c + gate[:, 3 * h + 1:3 * h + 2] * o_s
               + gate[:, 3 * h + 2:3 * h + 3] * o_w)
        o_ref[:, h * HEAD_DIM:(h + 1) * HEAD_DIM] = out.astype(o_ref.dtype)


def _selection_constants(seq):
    ncp = seq // CMP_STRIDE
    nc = ncp - CMP_LEN // CMP_STRIDE + 1
    nsel = seq // SEL_BLOCK
    ci = np.arange(ncp)[None, :] * CMP_STRIDE
    sj = np.arange(SEL_PAD)[:, None] * SEL_BLOCK
    cover_t = ((ci < sj + SEL_BLOCK) & (ci + CMP_LEN > sj)
               & (np.arange(ncp)[None, :] < nc) & (np.arange(SEL_PAD)[:, None] < nsel))
    key_block = (np.arange(seq)[:, None] // SEL_BLOCK) == np.arange(SEL_PAD)[None, :]
    return jnp.asarray(cover_t, BF16), jnp.asarray(key_block, BF16)


def _attention(za, kc, vct, gl, batch, seq, tq=512, tk=512):
    assert tk == WINDOW and tk % tq == 0 and tq & (tq - 1) == 0 and seq % (2 * tk) == 0
    ncp = seq // CMP_STRIDE
    nsel = seq // SEL_BLOCK
    nq = seq // tq
    cover_t, key_block = _selection_constants(seq)
    qcols = ATTN_W // LANE
    kern = functools.partial(_attn_kernel, tq=tq, tk=tk, ncp=ncp, nsel=nsel)
    kv_spec = lambda off: pl.BlockSpec((seq, HEAD_DIM), lambda b, g, i: (b, qcols + off * N_GROUP + g))
    return pl.pallas_call(
        kern,
        grid=(batch, N_GROUP, nq),
        in_specs=[
            pl.BlockSpec((tq, HPG * HEAD_DIM), lambda b, g, i: (b * nq + i, g)),
            kv_spec(0), kv_spec(1), kv_spec(2), kv_spec(3),
            pl.BlockSpec((seq, SEL_PAD), lambda b, g, i: (0, 0)),
            pl.BlockSpec((None, None, ncp, HEAD_DIM), lambda b, g, i: (b, g, 0, 0)),
            pl.BlockSpec((None, None, HEAD_DIM, ncp), lambda b, g, i: (b, g, 0, 0)),
            pl.BlockSpec((SEL_PAD, ncp), lambda b, g, i: (0, 0)),
            pl.BlockSpec((None, tq, N_GATE), lambda b, g, i: (g, b * nq + i, 0)),
        ],
        out_specs=pl.BlockSpec((tq, HPG * HEAD_DIM), lambda b, g, i: (b * nq + i, g)),
        out_shape=jax.ShapeDtypeStruct((batch * seq, ATTN_W), BF16),
        scratch_shapes=[pltpu.VMEM((HPG * tq, HEAD_DIM + SEL_PAD), BF16)],
        compiler_params=_params(("arbitrary", "arbitrary", "arbitrary")),
        name="nsa_attention",
    )(za, za, za, za, za, key_block, kc, vct, cover_t, gl)


def _lru_kernel(zg_ref, zr_ref, cw_ref, cb_ref, wa_ref, ba_ref, wx_ref, bx_ref, lam_ref, o_ref,
                tail_ref, h_ref, *, tt):
    @pl.when(pl.program_id(1) == 0)
    def _():
        tail_ref[...] = jnp.zeros_like(tail_ref)
        h_ref[...] = jnp.zeros_like(h_ref)

    zr = zr_ref[...]
    width = zr.shape[1]
    prev = tail_ref[...]
    xr = cw_ref[LRU_CONV - 1:LRU_CONV, :] * zr + cb_ref[...]
    for k in range(1, LRU_CONV):
        xr = xr + cw_ref[LRU_CONV - 1 - k:LRU_CONV - k, :] * _shift_rows(zr, prev, k)
    tail_ref[...] = zr[tt - SUBLANE:tt]

    xb = xr.astype(BF16)
    bw = width // LRU_BLOCKS
    r_parts, i_parts = [], []
    for n in range(LRU_BLOCKS):
        xs = xb[:, n * bw:(n + 1) * bw]
        r_parts.append(jnp.dot(xs, wa_ref[n], preferred_element_type=F32))
        i_parts.append(jnp.dot(xs, wx_ref[n], preferred_element_type=F32))
    rg = jax.nn.sigmoid(jnp.concatenate(r_parts, axis=1) + ba_ref[...])
    ig = jax.nn.sigmoid(jnp.concatenate(i_parts, axis=1) + bx_ref[...])
    lam = lam_ref[...]
    softplus_neg = jnp.maximum(-lam, 0.0) + jnp.log1p(jnp.exp(-jnp.abs(lam)))
    log_a = (-LRU_C) * rg * softplus_neg
    a = jnp.exp(log_a)
    b = jnp.sqrt(1.0 - a * a) * (ig * xr)

    row_in_group = lax.broadcasted_iota(jnp.int32, (tt, width), 0) & (SUBLANE - 1)
    d = 1
    while d < SUBLANE:
        a_s = pltpu.roll(a, d, 0)
        b_s = pltpu.roll(b, d, 0)
        keep = row_in_group >= d
        b = jnp.where(keep, a * b_s + b, b)
        a = jnp.where(keep, a * a_s, a)
        d *= 2
    h_prev = h_ref[...]
    groups = []
    for j in range(tt // SUBLANE):
        rs = slice(j * SUBLANE, (j + 1) * SUBLANE)
        h_j = a[rs] * h_prev + b[rs]
        groups.append(h_j)
        h_prev = h_j[SUBLANE - 1:SUBLANE]
    h = jnp.concatenate(groups, axis=0)
    h_ref[...] = h_prev
    o_ref[...] = (_gelu(zg_ref[...]) * h).astype(o_ref.dtype)


def _lru(zb, conv_w, conv_b, wa, ba, wx, bx, lam, batch, seq, tt=256):
    nt = seq // tt
    cg, cr = ZB_G // LRU_W, ZB_R // LRU_W
    kern = functools.partial(_lru_kernel, tt=tt)
    vec = lambda: pl.BlockSpec((1, LRU_W), lambda b, t: (0, 0))
    mat = lambda: pl.BlockSpec((LRU_BLOCKS, LRU_W // LRU_BLOCKS, LRU_W // LRU_BLOCKS), lambda b, t: (0, 0, 0))
    return pl.pallas_call(
        kern,
        grid=(batch, nt),
        in_specs=[
            pl.BlockSpec((tt, LRU_W), lambda b, t: (b * nt + t, cg)),
            pl.BlockSpec((tt, LRU_W), lambda b, t: (b * nt + t, cr)),
            pl.BlockSpec((LRU_CONV, LRU_W), lambda b, t: (0, 0)),
            vec(), mat(), vec(), mat(), vec(), vec(),
        ],
        out_specs=pl.BlockSpec((tt, LRU_W), lambda b, t: (b * nt + t, 0)),
        out_shape=jax.ShapeDtypeStruct((batch * seq, LRU_W), BF16),
        scratch_shapes=[pltpu.VMEM((SUBLANE, LRU_W), F32), pltpu.VMEM((1, LRU_W), F32)],
        compiler_params=_params(("arbitrary", "arbitrary")),
        name="rg_lru",
    )(zb, zb, conv_w, conv_b.reshape(1, LRU_W), wa.astype(BF16), ba.reshape(1, LRU_W),
      wx.astype(BF16), bx.reshape(1, LRU_W), lam.reshape(1, LRU_W))


def _sgu_kernel(zu_ref, zv_ref, g_ref, b_ref, ws_ref, bst_ref, o_ref, *, nchunk):
    v = _layer_norm(_gelu(zv_ref[...]), g_ref[...], b_ref[...]).astype(BF16)
    u = _gelu(zu_ref[...])
    gw = GMLP_W // GMLP_GROUPS
    tril = (lax.broadcasted_iota(jnp.int32, (CHUNK, CHUNK), 0)
            >= lax.broadcasted_iota(jnp.int32, (CHUNK, CHUNK), 1))
    for gi in range(GMLP_GROUPS):
        cs = slice(gi * gw, (gi + 1) * gw)
        w = jnp.where(tril, ws_ref[gi], 0.0).astype(BF16)
        rhs = jnp.concatenate([v[c * CHUNK:(c + 1) * CHUNK, cs] for c in range(nchunk)], axis=1)
        y = jnp.dot(w, rhs, preferred_element_type=F32) + bst_ref[:, gi:gi + 1]
        for c in range(nchunk):
            rs = slice(c * CHUNK, (c + 1) * CHUNK)
            o_ref[rs, cs] = (u[rs, cs] * y[:, c * gw:(c + 1) * gw]).astype(o_ref.dtype)


def _sgu(zb, ln_g, ln_b, w_s, b_s, batch, seq, nchunk=4):
    tt = nchunk * CHUNK
    nt = batch * seq // tt
    cu, cv = ZB_U // GMLP_W, ZB_V // GMLP_W
    kern = functools.partial(_sgu_kernel, nchunk=nchunk)
    return pl.pallas_call(
        kern,
        grid=(nt,),
        in_specs=[
            pl.BlockSpec((tt, GMLP_W), lambda t: (t, cu)),
            pl.BlockSpec((tt, GMLP_W), lambda t: (t, cv)),
            pl.BlockSpec((1, GMLP_W), lambda t: (0, 0)),
            pl.BlockSpec((1, GMLP_W), lambda t: (0, 0)),
            pl.BlockSpec((GMLP_GROUPS, CHUNK, CHUNK), lambda t: (0, 0, 0)),
            pl.BlockSpec((CHUNK, GMLP_GROUPS), lambda t: (0, 0)),
        ],
        out_specs=pl.BlockSpec((tt, GMLP_W), lambda t: (t, 0)),
        out_shape=jax.ShapeDtypeStruct((batch * seq, GMLP_W), BF16),
        compiler_params=_params(("arbitrary",)),
        name="spatial_gating",
    )(zb, zb, ln_g.reshape(1, GMLP_W), ln_b.reshape(1, GMLP_W), w_s, b_s.T)


def _oproj_kernel(ya_ref, yl_ref, ys_ref, w_ref, x_ref, gm_ref, lng_ref, lnb_ref, o_ref, *, na):
    k = pl.program_id(1)

    @pl.when(k == 0)
    def _():
        _zero_rows(o_ref)

    lhs = jnp.where(k < na, ya_ref[...], jnp.where(k == na, yl_ref[...], ys_ref[...]))
    _accumulate_matmul(o_ref, lhs, w_ref)

    @pl.when(k == pl.num_programs(1) - 1)
    def _():
        _deepnorm_epilogue(o_ref, x_ref, gm_ref, lng_ref, lnb_ref)


def _oproj(y_attn, y_lru, y_sgu, w_all, layer, x, mod_l, gate_idx, ln_g, ln_b, rows_per_batch, tm=512):
    m = y_attn.shape[0]
    tk = y_lru.shape[1]
    assert y_sgu.shape[1] == tk and y_attn.shape[1] % tk == 0
    na = y_attn.shape[1] // tk
    _, kdim, d = w_all.shape
    assert kdim == (na + 2) * tk
    tpb = rows_per_batch // tm
    return pl.pallas_call(
        functools.partial(_oproj_kernel, na=na),
        grid=(m // tm, na + 2),
        in_specs=[
            pl.BlockSpec((tm, tk), lambda i, k: (i, jnp.minimum(k, na - 1))),
            pl.BlockSpec((tm, tk), lambda i, k: (i, 0)),
            pl.BlockSpec((tm, tk), lambda i, k: (i, 0)),
            pl.BlockSpec((None, tk, d), lambda i, k: (layer, k, 0)),
            pl.BlockSpec((tm, d), lambda i, k: (i, 0)),
            pl.BlockSpec((None, 1, d), lambda i, k: (i // tpb, 0, gate_idx)),
            pl.BlockSpec((1, d), lambda i, k: (0, 0)),
            pl.BlockSpec((1, d), lambda i, k: (0, 0)),
        ],
        out_specs=pl.BlockSpec((tm, d), lambda i, k: (i, 0)),
        out_shape=jax.ShapeDtypeStruct((m, d), F32),
        compiler_params=_params(("arbitrary", "arbitrary")),
        name="out_proj_ln",
    )(y_attn, y_lru, y_sgu, w_all, x, mod_l, ln_g.reshape(1, d), ln_b.reshape(1, d))


def _ffn_kernel(x_ref, sc_ref, sh_ref, gm_ref, wg_ref, wu_ref, cw_ref, cb_ref, wd_ref, wd_last_ref,
                lng_ref, lnb_ref, o_ref, h_ref, tail_ref, act_ref, *, tm, tpb, nf):
    i = pl.program_id(0)
    f = pl.program_id(1)

    @pl.when(f == 0)
    def _():
        _modulate_into(h_ref, x_ref, sc_ref, sh_ref)
        _zero_rows(o_ref)
        act_ref[1] = jnp.zeros(act_ref.shape[1:], act_ref.dtype)

    @pl.when((f == 0) & (i == 0))
    def _():
        tail_ref[...] = jnp.zeros(tail_ref.shape, tail_ref.dtype)

    h = h_ref[...]
    g = jnp.dot(h, wg_ref[...], preferred_element_type=F32)
    u = jnp.dot(h, wu_ref[...], preferred_element_type=F32)
    _accumulate_matmul(o_ref, act_ref[(f + 1) % 2], wd_ref)

    prev = jnp.where(i % tpb == 0, 0.0, tail_ref[f])
    tail_ref[f] = g[tm - SUBLANE:tm]
    gc = cw_ref[FFN_CONV - 1:FFN_CONV, :] * g + cb_ref[...]
    for k in range(1, FFN_CONV):
        gc = gc + cw_ref[FFN_CONV - 1 - k:FFN_CONV - k, :] * _shift_rows(g, prev, k)
    act_ref[f % 2] = (gc * jax.nn.sigmoid(gc) * u).astype(BF16)

    @pl.when(f == nf - 1)
    def _():
        _accumulate_matmul(o_ref, act_ref[(nf - 1) % 2], wd_last_ref)
        _deepnorm_epilogue(o_ref, x_ref, gm_ref, lng_ref, lnb_ref)


def _ffn(x, mod_l, sc_idx, sh_idx, gate_idx, w_in, w_down, layer, conv_w, conv_b, ln_g, ln_b, rows_per_batch,
         tm=512, tf=256):
    m, d = x.shape
    dff = w_down.shape[1]
    nf = dff // tf
    tpb = rows_per_batch // tm
    kern = functools.partial(_ffn_kernel, tm=tm, tpb=tpb, nf=nf)
    prv = lambda f: jnp.maximum(f - 1, 0)
    return pl.pallas_call(
        kern,
        grid=(m // tm, nf),
        in_specs=[
            pl.BlockSpec((tm, d), lambda i, f: (i, 0)),
            pl.BlockSpec((None, 1, d), lambda i, f: (i // tpb, 0, sc_idx)),
            pl.BlockSpec((None, 1, d), lambda i, f: (i // tpb, 0, sh_idx)),
            pl.BlockSpec((None, 1, d), lambda i, f: (i // tpb, 0, gate_idx)),
            pl.BlockSpec((None, d, tf), lambda i, f: (layer, 0, f)),
            pl.BlockSpec((None, d, tf), lambda i, f: (layer, 0, nf + f)),
            pl.BlockSpec((FFN_CONV, tf), lambda i, f: (0, f)),
            pl.BlockSpec((1, tf), lambda i, f: (0, f)),
            pl.BlockSpec((None, tf, d), lambda i, f: (layer, prv(f), 0)),
            pl.BlockSpec((None, tf, d), lambda i, f: (layer, nf - 1, 0), pipeline_mode=pl.Buffered(1)),
            pl.BlockSpec((1, d), lambda i, f: (0, 0)),
            pl.BlockSpec((1, d), lambda i, f: (0, 0)),
        ],
        out_specs=pl.BlockSpec((tm, d), lambda i, f: (i, 0)),
        out_shape=jax.ShapeDtypeStruct((m, d), F32),
        scratch_shapes=[pltpu.VMEM((tm, d), BF16), pltpu.VMEM((nf, SUBLANE, tf), F32),
                        pltpu.VMEM((2, tm, tf), BF16)],
        compiler_params=_params(("arbitrary", "arbitrary")),
        name="conv_ffn_ln",
    )(x, mod_l, mod_l, mod_l, w_in, w_in, conv_w, conv_b.reshape(1, dff), w_down, w_down,
      ln_g.reshape(1, d), ln_b.reshape(1, d))


def _prepare_w_in(w, tn=512):
    a0 = ATTN_W + 6 * KV_W
    a1 = a0 + N_GATE * N_GROUP
    pad = jnp.zeros((w.shape[0], ZB_COLS - ZB_GL - N_GATE * N_GROUP), w.dtype)
    wn = jnp.concatenate([w[:, :a0], w[:, a1:], w[:, a0:a1], pad], axis=1).astype(BF16)
    blk = lambda lo, hi: tuple(range(lo // tn, hi // tn))
    qb = blk(0, ATTN_W)
    kcvc = blk(ATTN_W, ATTN_W + 2 * KV_W)
    kv4 = blk(ATTN_W + 2 * KV_W, a0)
    rest = blk(a0, wn.shape[1])
    return wn, qb + kv4 + kcvc + rest, len(qb + kv4)


def kernel(x, c, w_mod, b_mod, w_in, cmp_pos, cmp_w1, cmp_b1, cmp_w2, lru_conv_w, lru_conv_b, lru_wa,
           lru_ba, lru_wx, lru_bx, lru_lambda, sgu_ln_g, sgu_ln_b, sgu_w, sgu_b, w_o, ln1_g, ln1_b,
           ffn_w_in, ffn_conv_w, ffn_conv_b, ffn_w_down, ln2_g, ln2_b):
    batch, seq, d = x.shape
    depth = w_mod.shape[0]
    m = batch * seq
    x2 = x.reshape(m, d)
    c_pad = jnp.pad(c, ((0, SUBLANE - batch), (0, 0)))
    mod = _modulation(c_pad, w_mod, b_mod).reshape(depth, SUBLANE, 1, 6 * d)
    w_o_b = w_o.astype(BF16)
    ffn_w_in_b = ffn_w_in.astype(BF16)
    ffn_w_down_b = ffn_w_down.astype(BF16)
    for l in range(depth):
        mod_l = mod[l]
        wn, order, na = _prepare_w_in(w_in[l])
        za, zb = _in_proj(x2, mod_l, 1, 0, wn, order, na, seq)
        cmp = _compress(zb, cmp_pos[l], cmp_w1[l], cmp_b1[l], cmp_w2[l], batch, seq)
        kc = cmp[0]
        vct = jnp.swapaxes(cmp[1], -1, -2)
        gl = zb[:, ZB_GL:ZB_GL + N_GATE * N_GROUP].reshape(m, N_GROUP, N_GATE).transpose(1, 0, 2)
        y_attn = _attention(za, kc, vct, gl, batch, seq)
        y_lru = _lru(zb, lru_conv_w[l], lru_conv_b[l], lru_wa[l], lru_ba[l], lru_wx[l], lru_bx[l],
                     lru_lambda[l], batch, seq)
        y_sgu = _sgu(zb, sgu_ln_g[l], sgu_ln_b[l], sgu_w[l], sgu_b[l], batch, seq)
        x2 = _oproj(y_attn, y_lru, y_sgu, w_o_b, l, x2, mod_l, 2, ln1_g[l], ln1_b[l], seq)
        x2 = _ffn(x2, mod_l, 4, 3, 5, ffn_w_in_b, ffn_w_down_b, l, ffn_conv_w[l], ffn_conv_b[l],
                  ln2_g[l], ln2_b[l], seq)
    return x2.reshape(batch, seq, d)
```

```python
import functools

import jax
import jax.numpy as jnp
import numpy as np
from jax import lax
from jax.experimental import pallas as pl
from jax.experimental.pallas import tpu as pltpu

F32 = jnp.float32
BF16 = jnp.bfloat16

DEPTH = 2
HEAD_DIM = 128
HPG = 4
N_GROUP = 4
ATTN_W = HPG * N_GROUP * HEAD_DIM
KV_W = N_GROUP * HEAD_DIM
LRU_W = 1024
LRU_BLOCKS = 8
LRU_CONV = 4
LRU_C = 8.0
GMLP_W = 1024
GMLP_GROUPS = 8
CHUNK = 128
CMP_LEN = 32
CMP_STRIDE = 16
SEL_BLOCK = 64
SEL_TOPK = 16
WINDOW = 512
FFN_CONV = 3
NEG = -1e30
FORCE = 1e4
LN_EPS = 1e-5
ALPHA = (2.0 * DEPTH) ** 0.25
SCALE = HEAD_DIM ** -0.5
EXP2_SCALE = SCALE * 1.4426950408889634
N_GATE = 3 * HPG
SEL_PAD = 128
CHAIN_ROWS = 512

V7X_VMEM_LIMIT_BYTES = 58 * 1024 * 1024
LANE = 128
SUBLANE = 8

ZB_KC, ZB_VC, ZB_G, ZB_R, ZB_U, ZB_V, ZB_GL = 0, 512, 1024, 2048, 3072, 4096, 5120
ZB_COLS = 5632


def _params(sem):
    return pltpu.CompilerParams(dimension_semantics=sem, vmem_limit_bytes=V7X_VMEM_LIMIT_BYTES)


def _gelu(x):
    return 0.5 * x * (1.0 + jnp.tanh(0.7978845608028654 * (x + 0.044715 * (x * x * x))))


def _layer_norm(v, g, b):
    mu = jnp.mean(v, axis=-1, keepdims=True)
    d = v - mu
    var = jnp.mean(d * d, axis=-1, keepdims=True)
    return d * lax.rsqrt(var + LN_EPS) * g + b


def _shift_rows(cur, prev_tail, k):
    r = pltpu.roll(cur, k, 0)
    rp = pltpu.roll(prev_tail, k, 0)
    row = lax.broadcasted_iota(jnp.int32, prev_tail.shape, 0)
    first = jnp.where(row < k, rp, r[0:SUBLANE])
    return jnp.concatenate([first, r[SUBLANE:]], axis=0)


def _modulate_into(h_ref, x_ref, sc_ref, sh_ref, row_chunk=128):
    scale = 1.0 + sc_ref[...]
    shift = sh_ref[...]

    def body(r, _):
        rs = pl.ds(pl.multiple_of(r * row_chunk, row_chunk), row_chunk)
        h_ref[rs, :] = (x_ref[rs, :] * scale + shift).astype(h_ref.dtype)
        return 0

    lax.fori_loop(0, h_ref.shape[0] // row_chunk, body, 0)


def _zero_rows(o_ref, row_chunk=128):
    def body(r, _):
        rs = pl.ds(pl.multiple_of(r * row_chunk, row_chunk), row_chunk)
        o_ref[rs, :] = jnp.zeros((row_chunk, o_ref.shape[1]), o_ref.dtype)
        return 0

    lax.fori_loop(0, o_ref.shape[0] // row_chunk, body, 0)


def _accumulate_matmul(o_ref, lhs, w_ref, col_chunk=1024):
    n = o_ref.shape[1]
    for c0 in range(0, n, col_chunk):
        cs = slice(c0, min(c0 + col_chunk, n))
        o_ref[:, cs] += jnp.dot(lhs, w_ref[:, cs], preferred_element_type=F32)


def _deepnorm_epilogue(o_ref, x_ref, gm_ref, lng_ref, lnb_ref, row_chunk=128):
    gate = 1.0 + gm_ref[...]
    lng = lng_ref[...]
    lnb = lnb_ref[...]

    def body(r, _):
        rs = pl.ds(pl.multiple_of(r * row_chunk, row_chunk), row_chunk)
        v = ALPHA * x_ref[rs, :] + gate * o_ref[rs, :]
        o_ref[rs, :] = _layer_norm(v, lng, lnb)
        return 0

    lax.fori_loop(0, o_ref.shape[0] // row_chunk, body, 0)


def _mod_kernel(c_ref, w_ref, b_ref, o_ref):
    c = c_ref[...]
    cs = (c * jax.nn.sigmoid(c)).astype(BF16)
    o_ref[...] = jnp.dot(cs, w_ref[...].astype(BF16), preferred_element_type=F32) + b_ref[...]


def _modulation(c_pad, w_mod, b_mod, tn=1024):
    depth, d, n = w_mod.shape
    rows = c_pad.shape[0]
    return pl.pallas_call(
        _mod_kernel,
        grid=(depth, n // tn),
        in_specs=[
            pl.BlockSpec((rows, d), lambda l, j: (0, 0)),
            pl.BlockSpec((None, d, tn), lambda l, j: (l, 0, j)),
            pl.BlockSpec((None, 1, tn), lambda l, j: (l, 0, j)),
        ],
        out_specs=pl.BlockSpec((None, rows, tn), lambda l, j: (l, 0, j)),
        out_shape=jax.ShapeDtypeStruct((depth, rows, n), F32),
        compiler_params=_params(("arbitrary", "arbitrary")),
        name="modulation",
    )(c_pad, w_mod, b_mod.reshape(depth, 1, n))


def _in_proj_kernel(x_ref, sc_ref, sh_ref, w_ref, za_ref, zb_ref, h_ref, *, na):
    j = pl.program_id(1)

    @pl.when(j == 0)
    def _():
        _modulate_into(h_ref, x_ref, sc_ref, sh_ref)

    r = jnp.dot(h_ref[...], w_ref[...], preferred_element_type=F32)

    @pl.when(j < na)
    def _():
        za_ref[...] = r.astype(za_ref.dtype)

    @pl.when(j >= na)
    def _():
        zb_ref[...] = r.astype(zb_ref.dtype)


def _block_order_map(order):
    runs, start = [], 0
    for j in range(1, len(order) + 1):
        if j == len(order) or order[j] != order[j - 1] + 1:
            runs.append((j, order[start] - start))
            start = j

    def block_of(j):
        out = j + runs[-1][1]
        for end, off in reversed(runs[:-1]):
            out = jnp.where(j < end, j + off, out)
        return out

    return block_of


def _in_proj(x, mod_l, sc_idx, sh_idx, w, order, na, rows_per_batch, tm=1024, tn=512):
    m, k = x.shape
    nb = len(order)
    block_of = _block_order_map(order)
    tpb = rows_per_batch // tm
    kern = functools.partial(_in_proj_kernel, na=na)
    return pl.pallas_call(
        kern,
        grid=(m // tm, nb),
        in_specs=[
            pl.BlockSpec((tm, k), lambda i, j: (i, 0)),
            pl.BlockSpec((None, 1, k), lambda i, j: (i // tpb, 0, sc_idx)),
            pl.BlockSpec((None, 1, k), lambda i, j: (i // tpb, 0, sh_idx)),
            pl.BlockSpec((k, tn), lambda i, j: (0, block_of(j))),
        ],
        out_specs=[
            pl.BlockSpec((tm, tn), lambda i, j: (i, jnp.minimum(j, na - 1))),
            pl.BlockSpec((tm, tn), lambda i, j: (i, jnp.maximum(j - na, 0))),
        ],
        out_shape=[jax.ShapeDtypeStruct((m, na * tn), BF16),
                   jax.ShapeDtypeStruct((m, (nb - na) * tn), F32)],
        scratch_shapes=[pltpu.VMEM((tm, k), BF16)],
        compiler_params=_params(("arbitrary", "arbitrary")),
        name="in_proj",
    )(x, mod_l, mod_l, w)


def _compress_kernel(kv_ref, pos_ref, w1_ref, b1_ref, w2_ref, o_ref, *, ncp):
    a0 = jnp.zeros((ncp, HEAD_DIM), F32)
    a1 = jnp.zeros((ncp, HEAD_DIM), F32)
    for r in range(CMP_STRIDE):
        xr = kv_ref[pl.ds(r, ncp, stride=CMP_STRIDE), :]
        lo = (xr + pos_ref[r:r + 1, :]).astype(BF16)
        hi = (xr + pos_ref[CMP_STRIDE + r:CMP_STRIDE + r + 1, :]).astype(BF16)
        a0 = a0 + jnp.dot(lo, w1_ref[r * HEAD_DIM:(r + 1) * HEAD_DIM, :], preferred_element_type=F32)
        a1 = a1 + jnp.dot(hi, w1_ref[(CMP_STRIDE + r) * HEAD_DIM:(CMP_STRIDE + r + 1) * HEAD_DIM, :],
                          preferred_element_type=F32)
    pre = a0 + pltpu.roll(a1, ncp - 1, 0) + b1_ref[...]
    hid = _gelu(pre).astype(BF16)
    o_ref[...] = jnp.dot(hid, w2_ref[...], preferred_element_type=F32).astype(o_ref.dtype)


def _compress(zb, cmp_pos, cmp_w1, cmp_b1, cmp_w2, batch, seq):
    ncp = seq // CMP_STRIDE
    kern = functools.partial(_compress_kernel, ncp=ncp)
    return pl.pallas_call(
        kern,
        grid=(2, batch, N_GROUP),
        in_specs=[
            pl.BlockSpec((seq, HEAD_DIM), lambda w, b, g: (b, w * N_GROUP + g)),
            pl.BlockSpec((None, CMP_LEN, HEAD_DIM), lambda w, b, g: (w, 0, 0)),
            pl.BlockSpec((None, CMP_LEN * HEAD_DIM, HEAD_DIM), lambda w, b, g: (w, 0, 0)),
            pl.BlockSpec((None, 1, HEAD_DIM), lambda w, b, g: (w, 0, 0)),
            pl.BlockSpec((None, HEAD_DIM, HEAD_DIM), lambda w, b, g: (w, 0, 0)),
        ],
        out_specs=pl.BlockSpec((None, None, None, ncp, HEAD_DIM), lambda w, b, g: (w, b, g, 0, 0)),
        out_shape=jax.ShapeDtypeStruct((2, batch, N_GROUP, ncp, HEAD_DIM), BF16),
        compiler_params=_params(("arbitrary", "arbitrary", "arbitrary")),
        name="nsa_compress",
    )(zb, cmp_pos, cmp_w1.astype(BF16), cmp_b1.reshape(2, 1, HEAD_DIM), cmp_w2.astype(BF16))


_NT = (((1,), (1,)), ((), ()))


def _attn_kernel(q_ref, ks_ref, vs_ref, kw_ref, vw_ref, ke_ref, kc_ref, vct_ref, cov_ref, gl_ref,
                 o_ref, qa_ref, *, tq, tk, ncp, nsel):
    half = min(HPG * tq, CHAIN_ROWS)
    nch = HPG * tq // half
    hpc = half // tq
    q0 = pl.program_id(2) * tq
    for h in range(HPG):
        qa_ref[h * tq:(h + 1) * tq, 0:HEAD_DIM] = q_ref[:, h * HEAD_DIM:(h + 1) * HEAD_DIM]

    ci = lax.broadcasted_iota(jnp.int32, (ncp, half), 0) * CMP_STRIDE + (CMP_LEN - 1)
    tc = lax.broadcasted_iota(jnp.int32, (ncp, half), 1) & (tq - 1)
    valid_c = ci - tc <= q0
    t_row = q0 + (lax.broadcasted_iota(jnp.int32, (1, half), 1) & (tq - 1))
    any_c = (t_row >= CMP_LEN - 1).astype(F32)
    o_ct, p_sum = [], None
    for c in range(nch):
        qc = qa_ref[c * half:(c + 1) * half, 0:HEAD_DIM]
        s_c = lax.dot_general(kc_ref[...], qc, _NT, preferred_element_type=F32)
        s_c = jnp.where(valid_c, s_c, NEG)
        m_c = jnp.max(s_c, axis=0, keepdims=True)
        p_c = jnp.exp2((s_c - m_c) * EXP2_SCALE)
        l_c = jnp.sum(p_c, axis=0, keepdims=True)
        p_c = p_c * (any_c / l_c)
        o_ct.append(jnp.dot(vct_ref[...], p_c.astype(BF16), preferred_element_type=F32))
        for h in range(hpc):
            ph = p_c[:, h * tq:(h + 1) * tq]
            p_sum = ph if p_sum is None else p_sum + ph

    a = q0 // tk
    ts = 2 * tk

    def rel(width):
        return ((lax.broadcasted_iota(jnp.int32, (half, width), 0) & (tq - 1))
                - lax.broadcasted_iota(jnp.int32, (half, width), 1))

    ones_col = jnp.where(lax.broadcasted_iota(jnp.int32, (ts, LANE), 1) == 0, 1.0, 0.0).astype(BF16)

    def with_ones(v):
        return jnp.concatenate([v, ones_col], axis=1)

    def online(carry, s, v_aug):
        m, acc = carry
        mn = jnp.maximum(m, jnp.max(s, axis=1, keepdims=True))
        al = jnp.exp2((m - mn) * EXP2_SCALE)
        p = jnp.exp2((s - mn) * EXP2_SCALE).astype(BF16)
        acc = al * acc + jnp.dot(p, v_aug, preferred_element_type=F32)
        return mn, acc

    def normalised(state, sl):
        acc = state[1]
        return acc[sl, 0:HEAD_DIM] * (1.0 / acc[sl, HEAD_DIM:HEAD_DIM + 1])

    init1 = (jnp.full((half, 1), NEG, F32), jnp.zeros((half, HEAD_DIM + LANE), F32))

    w0 = pl.multiple_of(jnp.maximum(a - 1, 0) * tk, tk)
    kw = kw_ref[pl.ds(w0, ts), :]
    vw = with_ones(vw_ref[pl.ds(w0, ts), :])
    rel_w = rel(ts)
    win = []
    for c in range(nch):
        s = lax.dot_general(qa_ref[c * half:(c + 1) * half, 0:HEAD_DIM], kw, _NT, preferred_element_type=F32)
        s = jnp.where(rel_w >= w0 - q0, jnp.where(rel_w < WINDOW + w0 - q0, s, NEG), NEG)
        win.append(online(init1, s, vw))

    imp = jnp.dot(cov_ref[...], p_sum.astype(BF16), preferred_element_type=F32).T
    jb = lax.broadcasted_iota(jnp.int32, (tq, SEL_PAD), 1)
    tl = q0 + lax.broadcasted_iota(jnp.int32, (tq, SEL_PAD), 0)
    cur = lax.shift_right_logical(tl, 6)
    forced = (jb == 0) | (jb == cur) | (jb == cur - 1)
    imp = jnp.where(forced, FORCE, jnp.where(jb * SEL_BLOCK <= tl, imp, NEG))
    if nsel < SEL_PAD:
        imp = jnp.where(jb < nsel, imp, -jnp.inf)
    selb = jnp.full((tq, SEL_PAD), NEG, F32)
    for _ in range(min(SEL_TOPK, nsel)):
        hit = jb == jnp.argmax(imp, axis=1, keepdims=True)
        selb = jnp.where(hit, 0.0, selb)
        imp = jnp.where(hit, -jnp.inf, imp)
    selb_b = selb.astype(BF16)
    for h in range(HPG):
        qa_ref[h * tq:(h + 1) * tq, HEAD_DIM:HEAD_DIM + SEL_PAD] = selb_b

    def sel_tile(kt, carry, causal):
        start = pl.multiple_of(kt * ts, ts)
        kaug = jnp.concatenate([ks_ref[pl.ds(start, ts), :], ke_ref[pl.ds(start, ts), :]], axis=1)
        v = with_ones(vs_ref[pl.ds(start, ts), :])
        out = []
        for c in range(nch):
            s = lax.dot_general(qa_ref[c * half:(c + 1) * half, :], kaug, _NT, preferred_element_type=F32)
            if causal:
                s = jnp.where(rel(ts) >= start - q0, s, NEG)
            out.append(online(carry[c], s, v))
        return tuple(out)

    n_full = a // 2
    carry = lax.fori_loop(0, n_full, lambda kt, cr: sel_tile(kt, cr, False), (init1,) * nch)
    sel = sel_tile(n_full, carry, True)

    gate = jax.nn.sigmoid(gl_ref[...])
    for h in range(HPG):
        c, hh = divmod(h, hpc)
        sl = slice(hh * tq, (hh + 1) * tq)
        o_c = o_ct[c][:, sl].T
        o_s = normalised(sel[c], sl)
        o_w = normalised(win[c], sl)
        out = (gate[:, 3 * h:3 * h + 1] * o_c + gate[:, 3 * h + 1:3 * h + 2] * o_s
               + gate[:, 3 * h + 2:3 * h + 3] * o_w)
        o_ref[:, h * HEAD_DIM:(h + 1) * HEAD_DIM] = out.astype(o_ref.dtype)


def _selection_constants(seq):
    ncp = seq // CMP_STRIDE
    nc = ncp - CMP_LEN // CMP_STRIDE + 1
    nsel = seq // SEL_BLOCK
    ci = np.arange(ncp)[None, :] * CMP_STRIDE
    sj = np.arange(SEL_PAD)[:, None] * SEL_BLOCK
    cover_t = ((ci < sj + SEL_BLOCK) & (ci + CMP_LEN > sj)
               & (np.arange(ncp)[None, :] < nc) & (np.arange(SEL_PAD)[:, None] < nsel))
    key_block = (np.arange(seq)[:, None] // SEL_BLOCK) == np.arange(SEL_PAD)[None, :]
    return jnp.asarray(cover_t, BF16), jnp.asarray(key_block, BF16)


def _attention(za, kc, vct, gl, batch, seq, tq=512, tk=512):
    assert tk == WINDOW and tk % tq == 0 and tq & (tq - 1) == 0 and seq % (2 * tk) == 0
    ncp = seq // CMP_STRIDE
    nsel = seq // SEL_BLOCK
    nq = seq // tq
    cover_t, key_block = _selection_constants(seq)
    qcols = ATTN_W // LANE
    kern = functools.partial(_attn_kernel, tq=tq, tk=tk, ncp=ncp, nsel=nsel)
    kv_spec = lambda off: pl.BlockSpec((seq, HEAD_DIM), lambda b, g, i: (b, qcols + off * N_GROUP + g))
    return pl.pallas_call(
        kern,
        grid=(batch, N_GROUP, nq),
        in_specs=[
            pl.BlockSpec((tq, HPG * HEAD_DIM), lambda b, g, i: (b * nq + i, g)),
            kv_spec(0), kv_spec(1), kv_spec(2), kv_spec(3),
            pl.BlockSpec((seq, SEL_PAD), lambda b, g, i: (0, 0)),
            pl.BlockSpec((None, None, ncp, HEAD_DIM), lambda b, g, i: (b, g, 0, 0)),
            pl.BlockSpec((None, None, HEAD_DIM, ncp), lambda b, g, i: (b, g, 0, 0)),
            pl.BlockSpec((SEL_PAD, ncp), lambda b, g, i: (0, 0)),
            pl.BlockSpec((None, tq, N_GATE), lambda b, g, i: (g, b * nq + i, 0)),
        ],
        out_specs=pl.BlockSpec((tq, HPG * HEAD_DIM), lambda b, g, i: (b * nq + i, g)),
        out_shape=jax.ShapeDtypeStruct((batch * seq, ATTN_W), BF16),
        scratch_shapes=[pltpu.VMEM((HPG * tq, HEAD_DIM + SEL_PAD), BF16)],
        compiler_params=_params(("arbitrary", "arbitrary", "arbitrary")),
        name="nsa_attention",
    )(za, za, za, za, za, key_block, kc, vct, cover_t, gl)


def _lru_kernel(zg_ref, zr_ref, cw_ref, cb_ref, wa_ref, ba_ref, wx_ref, bx_ref, lam_ref, o_ref,
                tail_ref, h_ref, *, tt):
    @pl.when(pl.program_id(1) == 0)
    def _():
        tail_ref[...] = jnp.zeros_like(tail_ref)
        h_ref[...] = jnp.zeros_like(h_ref)

    zr = zr_ref[...]
    width = zr.shape[1]
    prev = tail_ref[...]
    xr = cw_ref[LRU_CONV - 1:LRU_CONV, :] * zr + cb_ref[...]
    for k in range(1, LRU_CONV):
        xr = xr + cw_ref[LRU_CONV - 1 - k:LRU_CONV - k, :] * _shift_rows(zr, prev, k)
    tail_ref[...] = zr[tt - SUBLANE:tt]

    xb = xr.astype(BF16)
    bw = width // LRU_BLOCKS
    r_parts, i_parts = [], []
    for n in range(LRU_BLOCKS):
        xs = xb[:, n * bw:(n + 1) * bw]
        r_parts.append(jnp.dot(xs, wa_ref[n], preferred_element_type=F32))
        i_parts.append(jnp.dot(xs, wx_ref[n], preferred_element_type=F32))
    rg = jax.nn.sigmoid(jnp.concatenate(r_parts, axis=1) + ba_ref[...])
    ig = jax.nn.sigmoid(jnp.concatenate(i_parts, axis=1) + bx_ref[...])
    lam = lam_ref[...]
    softplus_neg = jnp.maximum(-lam, 0.0) + jnp.log1p(jnp.exp(-jnp.abs(lam)))
    log_a = (-LRU_C) * rg * softplus_neg
    a = jnp.exp(log_a)
    b = jnp.sqrt(1.0 - a * a) * (ig * xr)

    row_in_group = lax.broadcasted_iota(jnp.int32, (tt, width), 0) & (SUBLANE - 1)
    d = 1
    while d < SUBLANE:
        a_s = pltpu.roll(a, d, 0)
        b_s = pltpu.roll(b, d, 0)
        keep = row_in_group >= d
        b = jnp.where(keep, a * b_s + b, b)
        a = jnp.where(keep, a * a_s, a)
        d *= 2
    h_prev = h_ref[...]
    groups = []
    for j in range(tt // SUBLANE):
        rs = slice(j * SUBLANE, (j + 1) * SUBLANE)
        h_j = a[rs] * h_prev + b[rs]
        groups.append(h_j)
        h_prev = h_j[SUBLANE - 1:SUBLANE]
    h = jnp.concatenate(groups, axis=0)
    h_ref[...] = h_prev
    o_ref[...] = (_gelu(zg_ref[...]) * h).astype(o_ref.dtype)


def _lru(zb, conv_w, conv_b, wa, ba, wx, bx, lam, batch, seq, tt=256):
    nt = seq // tt
    cg, cr = ZB_G // LRU_W, ZB_R // LRU_W
    kern = functools.partial(_lru_kernel, tt=tt)
    vec = lambda: pl.BlockSpec((1, LRU_W), lambda b, t: (0, 0))
    mat = lambda: pl.BlockSpec((LRU_BLOCKS, LRU_W // LRU_BLOCKS, LRU_W // LRU_BLOCKS), lambda b, t: (0, 0, 0))
    return pl.pallas_call(
        kern,
        grid=(batch, nt),
        in_specs=[
            pl.BlockSpec((tt, LRU_W), lambda b, t: (b * nt + t, cg)),
            pl.BlockSpec((tt, LRU_W), lambda b, t: (b * nt + t, cr)),
            pl.BlockSpec((LRU_CONV, LRU_W), lambda b, t: (0, 0)),
            vec(), mat(), vec(), mat(), vec(), vec(),
        ],
        out_specs=pl.BlockSpec((tt, LRU_W), lambda b, t: (b * nt + t, 0)),
        out_shape=jax.ShapeDtypeStruct((batch * seq, LRU_W), BF16),
        scratch_shapes=[pltpu.VMEM((SUBLANE, LRU_W), F32), pltpu.VMEM((1, LRU_W), F32)],
        compiler_params=_params(("arbitrary", "arbitrary")),
        name="rg_lru",
    )(zb, zb, conv_w, conv_b.reshape(1, LRU_W), wa.astype(BF16), ba.reshape(1, LRU_W),
      wx.astype(BF16), bx.reshape(1, LRU_W), lam.reshape(1, LRU_W))


def _sgu_kernel(zu_ref, zv_ref, g_ref, b_ref, ws_ref, bst_ref, o_ref, *, nchunk):
    v = _layer_norm(_gelu(zv_ref[...]), g_ref[...], b_ref[...]).astype(BF16)
    u = _gelu(zu_ref[...])
    gw = GMLP_W // GMLP_GROUPS
    tril = (lax.broadcasted_iota(jnp.int32, (CHUNK, CHUNK), 0)
            >= lax.broadcasted_iota(jnp.int32, (CHUNK, CHUNK), 1))
    for gi in range(GMLP_GROUPS):
        cs = slice(gi * gw, (gi + 1) * gw)
        w = jnp.where(tril, ws_ref[gi], 0.0).astype(BF16)
        rhs = jnp.concatenate([v[c * CHUNK:(c + 1) * CHUNK, cs] for c in range(nchunk)], axis=1)
        y = jnp.dot(w, rhs, preferred_element_type=F32) + bst_ref[:, gi:gi + 1]
        for c in range(nchunk):
            rs = slice(c * CHUNK, (c + 1) * CHUNK)
            o_ref[rs, cs] = (u[rs, cs] * y[:, c * gw:(c + 1) * gw]).astype(o_ref.dtype)


def _sgu(zb, ln_g, ln_b, w_s, b_s, batch, seq, nchunk=4):
    tt = nchunk * CHUNK
    nt = batch * seq // tt
    cu, cv = ZB_U // GMLP_W, ZB_V // GMLP_W
    kern = functools.partial(_sgu_kernel, nchunk=nchunk)
    return pl.pallas_call(
        kern,
        grid=(nt,),
        in_specs=[
            pl.BlockSpec((tt, GMLP_W), lambda t: (t, cu)),
            pl.BlockSpec((tt, GMLP_W), lambda t: (t, cv)),
            pl.BlockSpec((1, GMLP_W), lambda t: (0, 0)),
            pl.BlockSpec((1, GMLP_W), lambda t: (0, 0)),
            pl.BlockSpec((GMLP_GROUPS, CHUNK, CHUNK), lambda t: (0, 0, 0)),
            pl.BlockSpec((CHUNK, GMLP_GROUPS), lambda t: (0, 0)),
        ],
        out_specs=pl.BlockSpec((tt, GMLP_W), lambda t: (t, 0)),
        out_shape=jax.ShapeDtypeStruct((batch * seq, GMLP_W), BF16),
        compiler_params=_params(("arbitrary",)),
        name="spatial_gating",
    )(zb, zb, ln_g.reshape(1, GMLP_W), ln_b.reshape(1, GMLP_W), w_s, b_s.T)


def _oproj_kernel(ya_ref, yl_ref, ys_ref, w_ref, x_ref, gm_ref, lng_ref, lnb_ref, o_ref, *, na):
    k = pl.program_id(1)

    @pl.when(k == 0)
    def _():
        _zero_rows(o_ref)

    lhs = jnp.where(k < na, ya_ref[...], jnp.where(k == na, yl_ref[...], ys_ref[...]))
    _accumulate_matmul(o_ref, lhs, w_ref)

    @pl.when(k == pl.num_programs(1) - 1)
    def _():
        _deepnorm_epilogue(o_ref, x_ref, gm_ref, lng_ref, lnb_ref)


def _oproj(y_attn, y_lru, y_sgu, w_all, layer, x, mod_l, gate_idx, ln_g, ln_b, rows_per_batch, tm=512):
    m = y_attn.shape[0]
    tk = y_lru.shape[1]
    assert y_sgu.shape[1] == tk and y_attn.shape[1] % tk == 0
    na = y_attn.shape[1] // tk
    _, kdim, d = w_all.shape
    assert kdim == (na + 2) * tk
    tpb = rows_per_batch // tm
    return pl.pallas_call(
        functools.partial(_oproj_kernel, na=na),
        grid=(m // tm, na + 2),
        in_specs=[
            pl.BlockSpec((tm, tk), lambda i, k: (i, jnp.minimum(k, na - 1))),
            pl.BlockSpec((tm, tk), lambda i, k: (i, 0)),
            pl.BlockSpec((tm, tk), lambda i, k: (i, 0)),
            pl.BlockSpec((None, tk, d), lambda i, k: (layer, k, 0)),
            pl.BlockSpec((tm, d), lambda i, k: (i, 0)),
            pl.BlockSpec((None, 1, d), lambda i, k: (i // tpb, 0, gate_idx)),
            pl.BlockSpec((1, d), lambda i, k: (0, 0)),
            pl.BlockSpec((1, d), lambda i, k: (0, 0)),
        ],
        out_specs=pl.BlockSpec((tm, d), lambda i, k: (i, 0)),
        out_shape=jax.ShapeDtypeStruct((m, d), F32),
        compiler_params=_params(("arbitrary", "arbitrary")),
        name="out_proj_ln",
    )(y_attn, y_lru, y_sgu, w_all, x, mod_l, ln_g.reshape(1, d), ln_b.reshape(1, d))


def _ffn_kernel(x_ref, sc_ref, sh_ref, gm_ref, wg_ref, wu_ref, cw_ref, cb_ref, wd_ref, wd_last_ref,
                lng_ref, lnb_ref, o_ref, h_ref, tail_ref, act_ref, *, tm, tpb, nf):
    i = pl.program_id(0)
    f = pl.program_id(1)

    @pl.when(f == 0)
    def _():
        _modulate_into(h_ref, x_ref, sc_ref, sh_ref)
        _zero_rows(o_ref)
        act_ref[1] = jnp.zeros(act_ref.shape[1:], act_ref.dtype)

    @pl.when((f == 0) & (i == 0))
    def _():
        tail_ref[...] = jnp.zeros(tail_ref.shape, tail_ref.dtype)

    h = h_ref[...]
    g = jnp.dot(h, wg_ref[...], preferred_element_type=F32)
    u = jnp.dot(h, wu_ref[...], preferred_element_type=F32)
    _accumulate_matmul(o_ref, act_ref[(f + 1) % 2], wd_ref)

    prev = jnp.where(i % tpb == 0, 0.0, tail_ref[f])
    tail_ref[f] = g[tm - SUBLANE:tm]
    gc = cw_ref[FFN_CONV - 1:FFN_CONV, :] * g + cb_ref[...]
    for k in range(1, FFN_CONV):
        gc = gc + cw_ref[FFN_CONV - 1 - k:FFN_CONV - k, :] * _shift_rows(g, prev, k)
    act_ref[f % 2] = (gc * jax.nn.sigmoid(gc) * u).astype(BF16)

    @pl.when(f == nf - 1)
    def _():
        _accumulate_matmul(o_ref, act_ref[(nf - 1) % 2], wd_last_ref)
        _deepnorm_epilogue(o_ref, x_ref, gm_ref, lng_ref, lnb_ref)


def _ffn(x, mod_l, sc_idx, sh_idx, gate_idx, w_in, w_down, layer, conv_w, conv_b, ln_g, ln_b, rows_per_batch,
         tm=512, tf=256):
    m, d = x.shape
    dff = w_down.shape[1]
    nf = dff // tf
    tpb = rows_per_batch // tm
    kern = functools.partial(_ffn_kernel, tm=tm, tpb=tpb, nf=nf)
    prv = lambda f: jnp.maximum(f - 1, 0)
    return pl.pallas_call(
        kern,
        grid=(m // tm, nf),
        in_specs=[
            pl.BlockSpec((tm, d), lambda i, f: (i, 0)),
            pl.BlockSpec((None, 1, d), lambda i, f: (i // tpb, 0, sc_idx)),
            pl.BlockSpec((None, 1, d), lambda i, f: (i // tpb, 0, sh_idx)),
            pl.BlockSpec((None, 1, d), lambda i, f: (i // tpb, 0, gate_idx)),
            pl.BlockSpec((None, d, tf), lambda i, f: (layer, 0, f)),
            pl.BlockSpec((None, d, tf), lambda i, f: (layer, 0, nf + f)),
            pl.BlockSpec((FFN_CONV, tf), lambda i, f: (0, f)),
            pl.BlockSpec((1, tf), lambda i, f: (0, f)),
            pl.BlockSpec((None, tf, d), lambda i, f: (layer, prv(f), 0)),
            pl.BlockSpec((None, tf, d), lambda i, f: (layer, nf - 1, 0), pipeline_mode=pl.Buffered(1)),
            pl.BlockSpec((1, d), lambda i, f: (0, 0)),
            pl.BlockSpec((1, d), lambda i, f: (0, 0)),
        ],
        out_specs=pl.BlockSpec((tm, d), lambda i, f: (i, 0)),
        out_shape=jax.ShapeDtypeStruct((m, d), F32),
        scratch_shapes=[pltpu.VMEM((tm, d), BF16), pltpu.VMEM((nf, SUBLANE, tf), F32),
                        pltpu.VMEM((2, tm, tf), BF16)],
        compiler_params=_params(("arbitrary", "arbitrary")),
        name="conv_ffn_ln",
    )(x, mod_l, mod_l, mod_l, w_in, w_in, conv_w, conv_b.reshape(1, dff), w_down, w_down,
      ln_g.reshape(1, d), ln_b.reshape(1, d))


def _prepare_w_in(w, tn=512):
    a0 = ATTN_W + 6 * KV_W
    a1 = a0 + N_GATE * N_GROUP
    pad = jnp.zeros((w.shape[0], ZB_COLS - ZB_GL - N_GATE * N_GROUP), w.dtype)
    wn = jnp.concatenate([w[:, :a0], w[:, a1:], w[:, a0:a1], pad], axis=1).astype(BF16)
    blk = lambda lo, hi: tuple(range(lo // tn, hi // tn))
    qb = blk(0, ATTN_W)
    kcvc = blk(ATTN_W, ATTN_W + 2 * KV_W)
    kv4 = blk(ATTN_W + 2 * KV_W, a0)
    rest = blk(a0, wn.shape[1])
    return wn, qb + kv4 + kcvc + rest, len(qb + kv4)


def kernel(x, c, w_mod, b_mod, w_in, cmp_pos, cmp_w1, cmp_b1, cmp_w2, lru_conv_w, lru_conv_b, lru_wa,
           lru_ba, lru_wx, lru_bx, lru_lambda, sgu_ln_g, sgu_ln_b, sgu_w, sgu_b, w_o, ln1_g, ln1_b,
           ffn_w_in, ffn_conv_w, ffn_conv_b, ffn_w_down, ln2_g, ln2_b):
    batch, seq, d = x.shape
    depth = w_mod.shape[0]
    m = batch * seq
    x2 = x.reshape(m, d)
    c_pad = jnp.pad(c, ((0, SUBLANE - batch), (0, 0)))
    mod = _modulation(c_pad, w_mod, b_mod).reshape(depth, SUBLANE, 1, 6 * d)
    w_o_b = w_o.astype(BF16)
    ffn_w_in_b = ffn_w_in.astype(BF16)
    ffn_w_down_b = ffn_w_down.astype(BF16)
    w_in_b = w_in.astype(BF16)
    for l in range(depth):
        mod_l = mod[l]
        wn, order, na = _prepare_w_in(w_in_b[l])
        za, zb = _in_proj(x2, mod_l, 1, 0, wn, order, na, seq)
        cmp = _compress(zb, cmp_pos[l], cmp_w1[l], cmp_b1[l], cmp_w2[l], batch, seq)
        kc = cmp[0]
        vct = jnp.swapaxes(cmp[1], -1, -2)
        gl = zb[:, ZB_GL:ZB_GL + N_GATE * N_GROUP].reshape(m, N_GROUP, N_GATE).transpose(1, 0, 2)
        y_attn = _attention(za, kc, vct, gl, batch, seq)
        y_lru = _lru(zb, lru_conv_w[l], lru_conv_b[l], lru_wa[l], lru_ba[l], lru_wx[l], lru_bx[l],
                     lru_lambda[l], batch, seq)
        y_sgu = _sgu(zb, sgu_ln_g[l], sgu_ln_b[l], sgu_w[l], sgu_b[l], batch, seq)
        x2 = _oproj(y_attn, y_lru, y_sgu, w_o_b, l, x2, mod_l, 2, ln1_g[l], ln1_b[l], seq)
        x2 = _ffn(x2, mod_l, 4, 3, 5, ffn_w_in_b, ffn_w_down_b, l, ffn_conv_w[l], ffn_conv_b[l],
                  ln2_g[l], ln2_b[l], seq)
    return x2.reshape(batch, seq, d)
```

```python
import functools

import jax
import jax.numpy as jnp
import numpy as np
from jax import lax
from jax.experimental import pallas as pl
from jax.experimental.pallas import tpu as pltpu

F32 = jnp.float32
BF16 = jnp.bfloat16

DEPTH = 2
HEAD_DIM = 128
HPG = 4
N_GROUP = 4
ATTN_W = HPG * N_GROUP * HEAD_DIM
KV_W = N_GROUP * HEAD_DIM
LRU_W = 1024
LRU_BLOCKS = 8
LRU_CONV = 4
LRU_C = 8.0
GMLP_W = 1024
GMLP_GROUPS = 8
CHUNK = 128
CMP_LEN = 32
CMP_STRIDE = 16
SEL_BLOCK = 64
SEL_TOPK = 16
WINDOW = 512
FFN_CONV = 3
NEG = -1e30
FORCE = 1e4
LN_EPS = 1e-5
ALPHA = (2.0 * DEPTH) ** 0.25
SCALE = HEAD_DIM ** -0.5
EXP2_SCALE = SCALE * 1.4426950408889634
N_GATE = 3 * HPG
SEL_PAD = 128
CHAIN_ROWS = 512

V7X_VMEM_LIMIT_BYTES = 58 * 1024 * 1024
LANE = 128
SUBLANE = 8

ZB_KC, ZB_VC, ZB_G, ZB_R, ZB_U, ZB_V, ZB_GL = 0, 512, 1024, 2048, 3072, 4096, 5120
ZB_COLS = 5632


def _params(sem):
    return pltpu.CompilerParams(dimension_semantics=sem, vmem_limit_bytes=V7X_VMEM_LIMIT_BYTES)


def _gelu(x):
    return 0.5 * x * (1.0 + jnp.tanh(0.7978845608028654 * (x + 0.044715 * (x * x * x))))


def _layer_norm(v, g, b):
    mu = jnp.mean(v, axis=-1, keepdims=True)
    d = v - mu
    var = jnp.mean(d * d, axis=-1, keepdims=True)
    return d * lax.rsqrt(var + LN_EPS) * g + b


def _shift_rows(cur, prev_tail, k):
    r = pltpu.roll(cur, k, 0)
    rp = pltpu.roll(prev_tail, k, 0)
    row = lax.broadcasted_iota(jnp.int32, prev_tail.shape, 0)
    first = jnp.where(row < k, rp, r[0:SUBLANE])
    return jnp.concatenate([first, r[SUBLANE:]], axis=0)


def _modulate_into(h_ref, x_ref, sc_ref, sh_ref, row_chunk=128):
    scale = 1.0 + sc_ref[...]
    shift = sh_ref[...]

    def body(r, _):
        rs = pl.ds(pl.multiple_of(r * row_chunk, row_chunk), row_chunk)
        h_ref[rs, :] = (x_ref[rs, :] * scale + shift).astype(h_ref.dtype)
        return 0

    lax.fori_loop(0, h_ref.shape[0] // row_chunk, body, 0)


def _zero_rows(o_ref, row_chunk=128):
    def body(r, _):
        rs = pl.ds(pl.multiple_of(r * row_chunk, row_chunk), row_chunk)
        o_ref[rs, :] = jnp.zeros((row_chunk, o_ref.shape[1]), o_ref.dtype)
        return 0

    lax.fori_loop(0, o_ref.shape[0] // row_chunk, body, 0)


def _accumulate_matmul(o_ref, lhs, w_ref, col_chunk=1024):
    n = o_ref.shape[1]
    for c0 in range(0, n, col_chunk):
        cs = slice(c0, min(c0 + col_chunk, n))
        o_ref[:, cs] += jnp.dot(lhs, w_ref[:, cs], preferred_element_type=F32)


def _deepnorm_epilogue(o_ref, x_ref, gm_ref, lng_ref, lnb_ref, row_chunk=128):
    gate = 1.0 + gm_ref[...]
    lng = lng_ref[...]
    lnb = lnb_ref[...]

    def body(r, _):
        rs = pl.ds(pl.multiple_of(r * row_chunk, row_chunk), row_chunk)
        v = ALPHA * x_ref[rs, :] + gate * o_ref[rs, :]
        o_ref[rs, :] = _layer_norm(v, lng, lnb)
        return 0

    lax.fori_loop(0, o_ref.shape[0] // row_chunk, body, 0)


def _mod_kernel(c_ref, w_ref, b_ref, o_ref):
    c = c_ref[...]
    cs = (c * jax.nn.sigmoid(c)).astype(BF16)
    o_ref[...] = jnp.dot(cs, w_ref[...].astype(BF16), preferred_element_type=F32) + b_ref[...]


def _modulation(c_pad, w_mod, b_mod, tn=1024):
    depth, d, n = w_mod.shape
    rows = c_pad.shape[0]
    return pl.pallas_call(
        _mod_kernel,
        grid=(depth, n // tn),
        in_specs=[
            pl.BlockSpec((rows, d), lambda l, j: (0, 0)),
            pl.BlockSpec((None, d, tn), lambda l, j: (l, 0, j)),
            pl.BlockSpec((None, 1, tn), lambda l, j: (l, 0, j)),
        ],
        out_specs=pl.BlockSpec((None, rows, tn), lambda l, j: (l, 0, j)),
        out_shape=jax.ShapeDtypeStruct((depth, rows, n), F32),
        compiler_params=_params(("arbitrary", "arbitrary")),
        name="modulation",
    )(c_pad, w_mod, b_mod.reshape(depth, 1, n))


def _in_proj_kernel(x_ref, sc_ref, sh_ref, w_ref, za_ref, zb_ref, h_ref, *, na):
    j = pl.program_id(1)

    @pl.when(j == 0)
    def _():
        _modulate_into(h_ref, x_ref, sc_ref, sh_ref)

    r = jnp.dot(h_ref[...], w_ref[...], preferred_element_type=F32)

    @pl.when(j < na)
    def _():
        za_ref[...] = r.astype(za_ref.dtype)

    @pl.when(j >= na)
    def _():
        zb_ref[...] = r.astype(zb_ref.dtype)


def _block_order_map(order):
    runs, start = [], 0
    for j in range(1, len(order) + 1):
        if j == len(order) or order[j] != order[j - 1] + 1:
            runs.append((j, order[start] - start))
            start = j

    def block_of(j):
        out = j + runs[-1][1]
        for end, off in reversed(runs[:-1]):
            out = jnp.where(j < end, j + off, out)
        return out

    return block_of


def _in_proj(x, mod_l, sc_idx, sh_idx, w, order, na, rows_per_batch, tm=1024, tn=512):
    m, k = x.shape
    nb = len(order)
    block_of = _block_order_map(order)
    tpb = rows_per_batch // tm
    kern = functools.partial(_in_proj_kernel, na=na)
    return pl.pallas_call(
        kern,
        grid=(m // tm, nb),
        in_specs=[
            pl.BlockSpec((tm, k), lambda i, j: (i, 0)),
            pl.BlockSpec((None, 1, k), lambda i, j: (i // tpb, 0, sc_idx)),
            pl.BlockSpec((None, 1, k), lambda i, j: (i // tpb, 0, sh_idx)),
            pl.BlockSpec((k, tn), lambda i, j: (0, block_of(j))),
        ],
        out_specs=[
            pl.BlockSpec((tm, tn), lambda i, j: (i, jnp.minimum(j, na - 1))),
            pl.BlockSpec((tm, tn), lambda i, j: (i, jnp.maximum(j - na, 0))),
        ],
        out_shape=[jax.ShapeDtypeStruct((m, na * tn), BF16),
                   jax.ShapeDtypeStruct((m, (nb - na) * tn), F32)],
        scratch_shapes=[pltpu.VMEM((tm, k), BF16)],
        compiler_params=_params(("arbitrary", "arbitrary")),
        name="in_proj",
    )(x, mod_l, mod_l, w)


def _compress_kernel(kv_ref, pos_ref, w1_ref, b1_ref, w2_ref, o_ref, *, ncp):
    a0 = jnp.zeros((ncp, HEAD_DIM), F32)
    a1 = jnp.zeros((ncp, HEAD_DIM), F32)
    for r in range(CMP_STRIDE):
        xr = kv_ref[pl.ds(r, ncp, stride=CMP_STRIDE), :]
        lo = (xr + pos_ref[r:r + 1, :]).astype(BF16)
        hi = (xr + pos_ref[CMP_STRIDE + r:CMP_STRIDE + r + 1, :]).astype(BF16)
        a0 = a0 + jnp.dot(lo, w1_ref[r * HEAD_DIM:(r + 1) * HEAD_DIM, :], preferred_element_type=F32)
        a1 = a1 + jnp.dot(hi, w1_ref[(CMP_STRIDE + r) * HEAD_DIM:(CMP_STRIDE + r + 1) * HEAD_DIM, :],
                          preferred_element_type=F32)
    pre = a0 + pltpu.roll(a1, ncp - 1, 0) + b1_ref[...]
    hid = _gelu(pre).astype(BF16)
    o_ref[...] = jnp.dot(hid, w2_ref[...], preferred_element_type=F32).astype(o_ref.dtype)


def _compress(zb, cmp_pos, cmp_w1, cmp_b1, cmp_w2, batch, seq):
    ncp = seq // CMP_STRIDE
    kern = functools.partial(_compress_kernel, ncp=ncp)
    return pl.pallas_call(
        kern,
        grid=(2, batch, N_GROUP),
        in_specs=[
            pl.BlockSpec((seq, HEAD_DIM), lambda w, b, g: (b, w * N_GROUP + g)),
            pl.BlockSpec((None, CMP_LEN, HEAD_DIM), lambda w, b, g: (w, 0, 0)),
            pl.BlockSpec((None, CMP_LEN * HEAD_DIM, HEAD_DIM), lambda w, b, g: (w, 0, 0)),
            pl.BlockSpec((None, 1, HEAD_DIM), lambda w, b, g: (w, 0, 0)),
            pl.BlockSpec((None, HEAD_DIM, HEAD_DIM), lambda w, b, g: (w, 0, 0)),
        ],
        out_specs=pl.BlockSpec((None, None, None, ncp, HEAD_DIM), lambda w, b, g: (w, b, g, 0, 0)),
        out_shape=jax.ShapeDtypeStruct((2, batch, N_GROUP, ncp, HEAD_DIM), BF16),
        compiler_params=_params(("arbitrary", "arbitrary", "arbitrary")),
        name="nsa_compress",
    )(zb, cmp_pos, cmp_w1.astype(BF16), cmp_b1.reshape(2, 1, HEAD_DIM), cmp_w2.astype(BF16))


_NT = (((1,), (1,)), ((), ()))


def _attn_kernel(q_ref, ks_ref, vs_ref, kw_ref, vw_ref, ke_ref, kc_ref, vct_ref, cov_ref, gl_ref,
                 o_ref, qa_ref, *, tq, tk, ncp, nsel):
    half = min(HPG * tq, CHAIN_ROWS)
    nch = HPG * tq // half
    hpc = half // tq
    q0 = pl.program_id(2) * tq
    for h in range(HPG):
        qa_ref[h * tq:(h + 1) * tq, 0:HEAD_DIM] = q_ref[:, h * HEAD_DIM:(h + 1) * HEAD_DIM]

    ci = lax.broadcasted_iota(jnp.int32, (ncp, half), 0) * CMP_STRIDE + (CMP_LEN - 1)
    tc = lax.broadcasted_iota(jnp.int32, (ncp, half), 1) & (tq - 1)
    valid_c = ci - tc <= q0
    t_row = q0 + (lax.broadcasted_iota(jnp.int32, (1, half), 1) & (tq - 1))
    any_c = (t_row >= CMP_LEN - 1).astype(F32)
    o_ct, p_sum = [], None
    for c in range(nch):
        qc = qa_ref[c * half:(c + 1) * half, 0:HEAD_DIM]
        s_c = lax.dot_general(kc_ref[...], qc, _NT, preferred_element_type=F32)
        s_c = jnp.where(valid_c, s_c, NEG)
        m_c = jnp.max(s_c, axis=0, keepdims=True)
        p_c = jnp.exp2((s_c - m_c) * EXP2_SCALE)
        l_c = jnp.sum(p_c, axis=0, keepdims=True)
        p_c = p_c * (any_c / l_c)
        o_ct.append(jnp.dot(vct_ref[...], p_c.astype(BF16), preferred_element_type=F32))
        for h in range(hpc):
            ph = p_c[:, h * tq:(h + 1) * tq]
            p_sum = ph if p_sum is None else p_sum + ph

    a = q0 // tk
    ts = 2 * tk

    def rel(width):
        return ((lax.broadcasted_iota(jnp.int32, (half, width), 0) & (tq - 1))
                - lax.broadcasted_iota(jnp.int32, (half, width), 1))

    ones_col = jnp.where(lax.broadcasted_iota(jnp.int32, (ts, LANE), 1) == 0, 1.0, 0.0).astype(BF16)

    def with_ones(v):
        return jnp.concatenate([v, ones_col], axis=1)

    def online(carry, s, v_aug):
        m, acc = carry
        mn = jnp.maximum(m, jnp.max(s, axis=1, keepdims=True))
        al = jnp.exp2((m - mn) * EXP2_SCALE)
        p = jnp.exp2((s - mn) * EXP2_SCALE).astype(BF16)
        acc = al * acc + jnp.dot(p, v_aug, preferred_element_type=F32)
        return mn, acc

    def normalised(state, sl):
        acc = state[1]
        return acc[sl, 0:HEAD_DIM] * (1.0 / acc[sl, HEAD_DIM:HEAD_DIM + 1])

    init1 = (jnp.full((half, 1), NEG, F32), jnp.zeros((half, HEAD_DIM + LANE), F32))

    w0 = pl.multiple_of(jnp.maximum(a - 1, 0) * tk, tk)
    kw = kw_ref[pl.ds(w0, ts), :]
    vw = with_ones(vw_ref[pl.ds(w0, ts), :])
    rel_w = rel(ts)
    win = []
    for c in range(nch):
        s = lax.dot_general(qa_ref[c * half:(c + 1) * half, 0:HEAD_DIM], kw, _NT, preferred_element_type=F32)
        s = jnp.where(rel_w >= w0 - q0, jnp.where(rel_w < WINDOW + w0 - q0, s, NEG), NEG)
        win.append(online(init1, s, vw))

    imp = jnp.dot(cov_ref[...], p_sum.astype(BF16), preferred_element_type=F32).T
    jb = lax.broadcasted_iota(jnp.int32, (tq, SEL_PAD), 1)
    tl = q0 + lax.broadcasted_iota(jnp.int32, (tq, SEL_PAD), 0)
    cur = lax.shift_right_logical(tl, 6)
    forced = (jb == 0) | (jb == cur) | (jb == cur - 1)
    imp = jnp.where(forced, FORCE, jnp.where(jb * SEL_BLOCK <= tl, imp, NEG))
    if nsel < SEL_PAD:
        imp = jnp.where(jb < nsel, imp, -jnp.inf)
    selb = jnp.full((tq, SEL_PAD), NEG, F32)
    for _ in range(min(SEL_TOPK, nsel)):
        hit = jb == jnp.argmax(imp, axis=1, keepdims=True)
        selb = jnp.where(hit, 0.0, selb)
        imp = jnp.where(hit, -jnp.inf, imp)
    selb_b = selb.astype(BF16)
    for h in range(HPG):
        qa_ref[h * tq:(h + 1) * tq, HEAD_DIM:HEAD_DIM + SEL_PAD] = selb_b

    def sel_tile(kt, carry, causal):
        start = pl.multiple_of(kt * ts, ts)
        kaug = jnp.concatenate([ks_ref[pl.ds(start, ts), :], ke_ref[pl.ds(start, ts), :]], axis=1)
        v = with_ones(vs_ref[pl.ds(start, ts), :])
        out = []
        for c in range(nch):
            s = lax.dot_general(qa_ref[c * half:(c + 1) * half, :], kaug, _NT, preferred_element_type=F32)
            if causal:
                s = jnp.where(rel(ts) >= start - q0, s, NEG)
            out.append(online(carry[c], s, v))
        return tuple(out)

    n_full = a // 2
    carry = lax.fori_loop(0, n_full, lambda kt, cr: sel_tile(kt, cr, False), (init1,) * nch)
    sel = sel_tile(n_full, carry, True)

    gate = jax.nn.sigmoid(gl_ref[...])
    for h in range(HPG):
        c, hh = divmod(h, hpc)
        sl = slice(hh * tq, (hh + 1) * tq)
        o_c = o_ct[c][:, sl].T
        o_s = normalised(sel[c], sl)
        o_w = normalised(win[c], sl)
        out = (gate[:, 3 * h:3 * h + 1] * o_c + gate[:, 3 * h + 1:3 * h + 2] * o_s
               + gate[:, 3 * h + 2:3 * h + 3] * o_w)
        o_ref[:, h * HEAD_DIM:(h + 1) * HEAD_DIM] = out.astype(o_ref.dtype)


def _selection_constants(seq):
    ncp = seq // CMP_STRIDE
    nc = ncp - CMP_LEN // CMP_STRIDE + 1
    nsel = seq // SEL_BLOCK
    ci = np.arange(ncp)[None, :] * CMP_STRIDE
    sj = np.arange(SEL_PAD)[:, None] * SEL_BLOCK
    cover_t = ((ci < sj + SEL_BLOCK) & (ci + CMP_LEN > sj)
               & (np.arange(ncp)[None, :] < nc) & (np.arange(SEL_PAD)[:, None] < nsel))
    key_block = (np.arange(seq)[:, None] // SEL_BLOCK) == np.arange(SEL_PAD)[None, :]
    return jnp.asarray(cover_t, BF16), jnp.asarray(key_block, BF16)


def _attention(za, kc, vct, gl, batch, seq, tq=512, tk=512):
    assert tk == WINDOW and tk % tq == 0 and tq & (tq - 1) == 0 and seq % (2 * tk) == 0
    ncp = seq // CMP_STRIDE
    nsel = seq // SEL_BLOCK
    nq = seq // tq
    cover_t, key_block = _selection_constants(seq)
    qcols = ATTN_W // LANE
    kern = functools.partial(_attn_kernel, tq=tq, tk=tk, ncp=ncp, nsel=nsel)
    kv_spec = lambda off: pl.BlockSpec((seq, HEAD_DIM), lambda b, g, i: (b, qcols + off * N_GROUP + g))
    return pl.pallas_call(
        kern,
        grid=(batch, N_GROUP, nq),
        in_specs=[
            pl.BlockSpec((tq, HPG * HEAD_DIM), lambda b, g, i: (b * nq + i, g)),
            kv_spec(0), kv_spec(1), kv_spec(2), kv_spec(3),
            pl.BlockSpec((seq, SEL_PAD), lambda b, g, i: (0, 0)),
            pl.BlockSpec((None, None, ncp, HEAD_DIM), lambda b, g, i: (b, g, 0, 0)),
            pl.BlockSpec((None, None, HEAD_DIM, ncp), lambda b, g, i: (b, g, 0, 0)),
            pl.BlockSpec((SEL_PAD, ncp), lambda b, g, i: (0, 0)),
            pl.BlockSpec((None, tq, N_GATE), lambda b, g, i: (g, b * nq + i, 0)),
        ],
        out_specs=pl.BlockSpec((tq, HPG * HEAD_DIM), lambda b, g, i: (b * nq + i, g)),
        out_shape=jax.ShapeDtypeStruct((batch * seq, ATTN_W), BF16),
        scratch_shapes=[pltpu.VMEM((HPG * tq, HEAD_DIM + SEL_PAD), BF16)],
        compiler_params=_params(("arbitrary", "arbitrary", "arbitrary")),
        name="nsa_attention",
    )(za, za, za, za, za, key_block, kc, vct, cover_t, gl)


def _lru_kernel(zg_ref, zr_ref, cw_ref, cb_ref, wa_ref, ba_ref, wx_ref, bx_ref, lam_ref, o_ref,
                tail_ref, h_ref, *, tt):
    @pl.when(pl.program_id(1) == 0)
    def _():
        tail_ref[...] = jnp.zeros_like(tail_ref)
        h_ref[...] = jnp.zeros_like(h_ref)

    zr = zr_ref[...]
    width = zr.shape[1]
    prev = tail_ref[...]
    xr = cw_ref[LRU_CONV - 1:LRU_CONV, :] * zr + cb_ref[...]
    for k in range(1, LRU_CONV):
        xr = xr + cw_ref[LRU_CONV - 1 - k:LRU_CONV - k, :] * _shift_rows(zr, prev, k)
    tail_ref[...] = zr[tt - SUBLANE:tt]

    xb = xr.astype(BF16)
    bw = width // LRU_BLOCKS
    r_parts, i_parts = [], []
    for n in range(LRU_BLOCKS):
        xs = xb[:, n * bw:(n + 1) * bw]
        r_parts.append(jnp.dot(xs, wa_ref[n], preferred_element_type=F32))
        i_parts.append(jnp.dot(xs, wx_ref[n], preferred_element_type=F32))
    rg = jax.nn.sigmoid(jnp.concatenate(r_parts, axis=1) + ba_ref[...])
    ig = jax.nn.sigmoid(jnp.concatenate(i_parts, axis=1) + bx_ref[...])
    lam = lam_ref[...]
    softplus_neg = jnp.maximum(-lam, 0.0) + jnp.log1p(jnp.exp(-jnp.abs(lam)))
    log_a = (-LRU_C) * rg * softplus_neg
    a = jnp.exp(log_a)
    b = jnp.sqrt(1.0 - a * a) * (ig * xr)

    row_in_group = lax.broadcasted_iota(jnp.int32, (tt, width), 0) & (SUBLANE - 1)
    d = 1
    while d < SUBLANE:
        a_s = pltpu.roll(a, d, 0)
        b_s = pltpu.roll(b, d, 0)
        keep = row_in_group >= d
        b = jnp.where(keep, a * b_s + b, b)
        a = jnp.where(keep, a * a_s, a)
        d *= 2
    h_prev = h_ref[...]
    groups = []
    for j in range(tt // SUBLANE):
        rs = slice(j * SUBLANE, (j + 1) * SUBLANE)
        h_j = a[rs] * h_prev + b[rs]
        groups.append(h_j)
        h_prev = h_j[SUBLANE - 1:SUBLANE]
    h = jnp.concatenate(groups, axis=0)
    h_ref[...] = h_prev
    o_ref[...] = (_gelu(zg_ref[...]) * h).astype(o_ref.dtype)


def _lru(zb, conv_w, conv_b, wa, ba, wx, bx, lam, batch, seq, tt=256):
    nt = seq // tt
    cg, cr = ZB_G // LRU_W, ZB_R // LRU_W
    kern = functools.partial(_lru_kernel, tt=tt)
    vec = lambda: pl.BlockSpec((1, LRU_W), lambda b, t: (0, 0))
    mat = lambda: pl.BlockSpec((LRU_BLOCKS, LRU_W // LRU_BLOCKS, LRU_W // LRU_BLOCKS), lambda b, t: (0, 0, 0))
    return pl.pallas_call(
        kern,
        grid=(batch, nt),
        in_specs=[
            pl.BlockSpec((tt, LRU_W), lambda b, t: (b * nt + t, cg)),
            pl.BlockSpec((tt, LRU_W), lambda b, t: (b * nt + t, cr)),
            pl.BlockSpec((LRU_CONV, LRU_W), lambda b, t: (0, 0)),
            vec(), mat(), vec(), mat(), vec(), vec(),
        ],
        out_specs=pl.BlockSpec((tt, LRU_W), lambda b, t: (b * nt + t, 0)),
        out_shape=jax.ShapeDtypeStruct((batch * seq, LRU_W), BF16),
        scratch_shapes=[pltpu.VMEM((SUBLANE, LRU_W), F32), pltpu.VMEM((1, LRU_W), F32)],
        compiler_params=_params(("arbitrary", "arbitrary")),
        name="rg_lru",
    )(zb, zb, conv_w, conv_b.reshape(1, LRU_W), wa.astype(BF16), ba.reshape(1, LRU_W),
      wx.astype(BF16), bx.reshape(1, LRU_W), lam.reshape(1, LRU_W))


def _sgu_kernel(zu_ref, zv_ref, g_ref, b_ref, ws_ref, bst_ref, o_ref, *, nchunk):
    v = _layer_norm(_gelu(zv_ref[...]), g_ref[...], b_ref[...]).astype(BF16)
    u = _gelu(zu_ref[...])
    gw = GMLP_W // GMLP_GROUPS
    tril = (lax.broadcasted_iota(jnp.int32, (CHUNK, CHUNK), 0)
            >= lax.broadcasted_iota(jnp.int32, (CHUNK, CHUNK), 1))
    for gi in range(GMLP_GROUPS):
        cs = slice(gi * gw, (gi + 1) * gw)
        w = jnp.where(tril, ws_ref[gi], 0.0).astype(BF16)
        rhs = jnp.concatenate([v[c * CHUNK:(c + 1) * CHUNK, cs] for c in range(nchunk)], axis=1)
        y = jnp.dot(w, rhs, preferred_element_type=F32) + bst_ref[:, gi:gi + 1]
        for c in range(nchunk):
            rs = slice(c * CHUNK, (c + 1) * CHUNK)
            o_ref[rs, cs] = (u[rs, cs] * y[:, c * gw:(c + 1) * gw]).astype(o_ref.dtype)


def _sgu(zb, ln_g, ln_b, w_s, b_s, batch, seq, nchunk=4):
    tt = nchunk * CHUNK
    nt = batch * seq // tt
    cu, cv = ZB_U // GMLP_W, ZB_V // GMLP_W
    kern = functools.partial(_sgu_kernel, nchunk=nchunk)
    return pl.pallas_call(
        kern,
        grid=(nt,),
        in_specs=[
            pl.BlockSpec((tt, GMLP_W), lambda t: (t, cu)),
            pl.BlockSpec((tt, GMLP_W), lambda t: (t, cv)),
            pl.BlockSpec((1, GMLP_W), lambda t: (0, 0)),
            pl.BlockSpec((1, GMLP_W), lambda t: (0, 0)),
            pl.BlockSpec((GMLP_GROUPS, CHUNK, CHUNK), lambda t: (0, 0, 0)),
            pl.BlockSpec((CHUNK, GMLP_GROUPS), lambda t: (0, 0)),
        ],
        out_specs=pl.BlockSpec((tt, GMLP_W), lambda t: (t, 0)),
        out_shape=jax.ShapeDtypeStruct((batch * seq, GMLP_W), BF16),
        compiler_params=_params(("arbitrary",)),
        name="spatial_gating",
    )(zb, zb, ln_g.reshape(1, GMLP_W), ln_b.reshape(1, GMLP_W), w_s, b_s.T)


def _oproj_kernel(ya_ref, yl_ref, ys_ref, w_ref, x_ref, gm_ref, lng_ref, lnb_ref, o_ref, *, na):
    k = pl.program_id(1)

    @pl.when(k == 0)
    def _():
        _zero_rows(o_ref)

    lhs = jnp.where(k < na, ya_ref[...], jnp.where(k == na, yl_ref[...], ys_ref[...]))
    _accumulate_matmul(o_ref, lhs, w_ref)

    @pl.when(k == pl.num_programs(1) - 1)
    def _():
        _deepnorm_epilogue(o_ref, x_ref, gm_ref, lng_ref, lnb_ref)


def _oproj(y_attn, y_lru, y_sgu, w_all, layer, x, mod_l, gate_idx, ln_g, ln_b, rows_per_batch, tm=512):
    m = y_attn.shape[0]
    tk = y_lru.shape[1]
    assert y_sgu.shape[1] == tk and y_attn.shape[1] % tk == 0
    na = y_attn.shape[1] // tk
    _, kdim, d = w_all.shape
    assert kdim == (na + 2) * tk
    tpb = rows_per_batch // tm
    return pl.pallas_call(
        functools.partial(_oproj_kernel, na=na),
        grid=(m // tm, na + 2),
        in_specs=[
            pl.BlockSpec((tm, tk), lambda i, k: (i, jnp.minimum(k, na - 1))),
            pl.BlockSpec((tm, tk), lambda i, k: (i, 0)),
            pl.BlockSpec((tm, tk), lambda i, k: (i, 0)),
            pl.BlockSpec((None, tk, d), lambda i, k: (layer, k, 0)),
            pl.BlockSpec((tm, d), lambda i, k: (i, 0)),
            pl.BlockSpec((None, 1, d), lambda i, k: (i // tpb, 0, gate_idx)),
            pl.BlockSpec((1, d), lambda i, k: (0, 0)),
            pl.BlockSpec((1, d), lambda i, k: (0, 0)),
        ],
        out_specs=pl.BlockSpec((tm, d), lambda i, k: (i, 0)),
        out_shape=jax.ShapeDtypeStruct((m, d), F32),
        compiler_params=_params(("arbitrary", "arbitrary")),
        name="out_proj_ln",
    )(y_attn, y_lru, y_sgu, w_all, x, mod_l, ln_g.reshape(1, d), ln_b.reshape(1, d))


def _ffn_kernel(x_ref, mod_ref, wg_ref, wu_ref, conv_ref, wd_ref, wd_last_ref, ln_ref,
                o_ref, h_ref, tail_ref, act_ref, *, tm, tpb, nf, sc_idx, sh_idx, gate_idx):
    i = pl.program_id(0)
    f = pl.program_id(1)
    d = x_ref.shape[1]
    sc_ref, sh_ref, gm_ref = (mod_ref.at[:, c * d:(c + 1) * d] for c in (sc_idx, sh_idx, gate_idx))
    cw_ref, cb_ref = conv_ref.at[0:FFN_CONV], conv_ref.at[FFN_CONV:FFN_CONV + 1]
    lng_ref, lnb_ref = ln_ref.at[0:1], ln_ref.at[1:2]

    @pl.when(f == 0)
    def _():
        _modulate_into(h_ref, x_ref, sc_ref, sh_ref)
        _zero_rows(o_ref)
        act_ref[1] = jnp.zeros(act_ref.shape[1:], act_ref.dtype)

    @pl.when((f == 0) & (i == 0))
    def _():
        tail_ref[...] = jnp.zeros(tail_ref.shape, tail_ref.dtype)

    h = h_ref[...]
    g = jnp.dot(h, wg_ref[...], preferred_element_type=F32)
    u = jnp.dot(h, wu_ref[...], preferred_element_type=F32)
    _accumulate_matmul(o_ref, act_ref[(f + 1) % 2], wd_ref)

    prev = jnp.where(i % tpb == 0, 0.0, tail_ref[f])
    tail_ref[f] = g[tm - SUBLANE:tm]
    gc = cw_ref[FFN_CONV - 1:FFN_CONV, :] * g + cb_ref[...]
    for k in range(1, FFN_CONV):
        gc = gc + cw_ref[FFN_CONV - 1 - k:FFN_CONV - k, :] * _shift_rows(g, prev, k)
    act_ref[f % 2] = (gc * jax.nn.sigmoid(gc) * u).astype(BF16)

    @pl.when(f == nf - 1)
    def _():
        _accumulate_matmul(o_ref, act_ref[(nf - 1) % 2], wd_last_ref)
        _deepnorm_epilogue(o_ref, x_ref, gm_ref, lng_ref, lnb_ref)


def _ffn(x, mod_l, sc_idx, sh_idx, gate_idx, w_in, w_down, layer, conv_w, conv_b, ln_g, ln_b, rows_per_batch,
         tm=512, tf=256):
    m, d = x.shape
    dff = w_down.shape[1]
    nf = dff // tf
    tpb = rows_per_batch // tm
    kern = functools.partial(_ffn_kernel, tm=tm, tpb=tpb, nf=nf, sc_idx=sc_idx, sh_idx=sh_idx,
                             gate_idx=gate_idx)
    prv = lambda f: jnp.maximum(f - 1, 0)
    conv = jnp.concatenate([conv_w, conv_b.reshape(1, dff)], axis=0)
    ln = jnp.stack([ln_g, ln_b])
    return pl.pallas_call(
        kern,
        grid=(m // tm, nf),
        in_specs=[
            pl.BlockSpec((tm, d), lambda i, f: (i, 0)),
            pl.BlockSpec((None, 1, mod_l.shape[2]), lambda i, f: (i // tpb, 0, 0)),
            pl.BlockSpec((None, d, tf), lambda i, f: (layer, 0, f)),
            pl.BlockSpec((None, d, tf), lambda i, f: (layer, 0, nf + f)),
            pl.BlockSpec((FFN_CONV + 1, tf), lambda i, f: (0, f)),
            pl.BlockSpec((None, tf, d), lambda i, f: (layer, prv(f), 0)),
            pl.BlockSpec((None, tf, d), lambda i, f: (layer, nf - 1, 0), pipeline_mode=pl.Buffered(1)),
            pl.BlockSpec((2, d), lambda i, f: (0, 0)),
        ],
        out_specs=pl.BlockSpec((tm, d), lambda i, f: (i, 0)),
        out_shape=jax.ShapeDtypeStruct((m, d), F32),
        scratch_shapes=[pltpu.VMEM((tm, d), BF16), pltpu.VMEM((nf, SUBLANE, tf), F32),
                        pltpu.VMEM((2, tm, tf), BF16)],
        compiler_params=_params(("arbitrary", "arbitrary")),
        name="conv_ffn_ln",
    )(x, mod_l, w_in, w_in, conv, w_down, w_down, ln)


def _prepare_w_in(w, tn=512):
    a0 = ATTN_W + 6 * KV_W
    a1 = a0 + N_GATE * N_GROUP
    pad = jnp.zeros((w.shape[0], ZB_COLS - ZB_GL - N_GATE * N_GROUP), w.dtype)
    wn = jnp.concatenate([w[:, :a0], w[:, a1:], w[:, a0:a1], pad], axis=1).astype(BF16)
    blk = lambda lo, hi: tuple(range(lo // tn, hi // tn))
    qb = blk(0, ATTN_W)
    kcvc = blk(ATTN_W, ATTN_W + 2 * KV_W)
    kv4 = blk(ATTN_W + 2 * KV_W, a0)
    rest = blk(a0, wn.shape[1])
    return wn, qb + kv4 + kcvc + rest, len(qb + kv4)


def kernel(x, c, w_mod, b_mod, w_in, cmp_pos, cmp_w1, cmp_b1, cmp_w2, lru_conv_w, lru_conv_b, lru_wa,
           lru_ba, lru_wx, lru_bx, lru_lambda, sgu_ln_g, sgu_ln_b, sgu_w, sgu_b, w_o, ln1_g, ln1_b,
           ffn_w_in, ffn_conv_w, ffn_conv_b, ffn_w_down, ln2_g, ln2_b):
    batch, seq, d = x.shape
    depth = w_mod.shape[0]
    m = batch * seq
    x2 = x.reshape(m, d)
    c_pad = jnp.pad(c, ((0, SUBLANE - batch), (0, 0)))
    mod = _modulation(c_pad, w_mod, b_mod).reshape(depth, SUBLANE, 1, 6 * d)
    w_o_b = w_o.astype(BF16)
    ffn_w_in_b = ffn_w_in.astype(BF16)
    ffn_w_down_b = ffn_w_down.astype(BF16)
    for l in range(depth):
        mod_l = mod[l]
        wn, order, na = _prepare_w_in(w_in[l])
        za, zb = _in_proj(x2, mod_l, 1, 0, wn, order, na, seq)
        cmp = _compress(zb, cmp_pos[l], cmp_w1[l], cmp_b1[l], cmp_w2[l], batch, seq)
        kc = cmp[0]
        vct = jnp.swapaxes(cmp[1], -1, -2)
        gl = zb[:, ZB_GL:ZB_GL + N_GATE * N_GROUP].reshape(m, N_GROUP, N_GATE).transpose(1, 0, 2)
        y_attn = _attention(za, kc, vct, gl, batch, seq)
        y_lru = _lru(zb, lru_conv_w[l], lru_conv_b[l], lru_wa[l], lru_ba[l], lru_wx[l], lru_bx[l],
                     lru_lambda[l], batch, seq)
        y_sgu = _sgu(zb, sgu_ln_g[l], sgu_ln_b[l], sgu_w[l], sgu_b[l], batch, seq)
        x2 = _oproj(y_attn, y_lru, y_sgu, w_o_b, l, x2, mod_l, 2, ln1_g[l], ln1_b[l], seq)
        x2 = _ffn(x2, mod_l, 4, 3, 5, ffn_w_in_b, ffn_w_down_b, l, ffn_conv_w[l], ffn_conv_b[l],
                  ln2_g[l], ln2_b[l], seq)
    return x2.reshape(batch, seq, d)
```

```python
import functools

import jax
import jax.numpy as jnp
import numpy as np
from jax import lax
from jax.experimental import pallas as pl
from jax.experimental.pallas import tpu as pltpu

F32 = jnp.float32
BF16 = jnp.bfloat16

DEPTH = 2
HEAD_DIM = 128
HPG = 4
N_GROUP = 4
ATTN_W = HPG * N_GROUP * HEAD_DIM
KV_W = N_GROUP * HEAD_DIM
LRU_W = 1024
LRU_BLOCKS = 8
LRU_CONV = 4
LRU_C = 8.0
GMLP_W = 1024
GMLP_GROUPS = 8
CHUNK = 128
CMP_LEN = 32
CMP_STRIDE = 16
SEL_BLOCK = 64
SEL_TOPK = 16
WINDOW = 512
FFN_CONV = 3
NEG = -1e30
FORCE = 1e4
LN_EPS = 1e-5
ALPHA = (2.0 * DEPTH) ** 0.25
SCALE = HEAD_DIM ** -0.5
EXP2_SCALE = SCALE * 1.4426950408889634
N_GATE = 3 * HPG
SEL_PAD = 128
CHAIN_ROWS = 512

V7X_VMEM_LIMIT_BYTES = 58 * 1024 * 1024
LANE = 128
SUBLANE = 8

ZB_KC, ZB_VC, ZB_G, ZB_R, ZB_U, ZB_V, ZB_GL = 0, 512, 1024, 2048, 3072, 4096, 5120
ZB_COLS = 5632


def _params(sem):
    return pltpu.CompilerParams(dimension_semantics=sem, vmem_limit_bytes=V7X_VMEM_LIMIT_BYTES)


def _gelu(x):
    return 0.5 * x * (1.0 + jnp.tanh(0.7978845608028654 * (x + 0.044715 * (x * x * x))))


def _layer_norm(v, g, b):
    mu = jnp.mean(v, axis=-1, keepdims=True)
    d = v - mu
    var = jnp.mean(d * d, axis=-1, keepdims=True)
    return d * lax.rsqrt(var + LN_EPS) * g + b


def _shift_rows(cur, prev_tail, k):
    r = pltpu.roll(cur, k, 0)
    rp = pltpu.roll(prev_tail, k, 0)
    row = lax.broadcasted_iota(jnp.int32, prev_tail.shape, 0)
    first = jnp.where(row < k, rp, r[0:SUBLANE])
    return jnp.concatenate([first, r[SUBLANE:]], axis=0)


def _modulate_into(h_ref, x_ref, sc_ref, sh_ref, row_chunk=128):
    scale = 1.0 + sc_ref[...]
    shift = sh_ref[...]

    def body(r, _):
        rs = pl.ds(pl.multiple_of(r * row_chunk, row_chunk), row_chunk)
        h_ref[rs, :] = (x_ref[rs, :] * scale + shift).astype(h_ref.dtype)
        return 0

    lax.fori_loop(0, h_ref.shape[0] // row_chunk, body, 0)


def _zero_rows(o_ref, row_chunk=128):
    def body(r, _):
        rs = pl.ds(pl.multiple_of(r * row_chunk, row_chunk), row_chunk)
        o_ref[rs, :] = jnp.zeros((row_chunk, o_ref.shape[1]), o_ref.dtype)
        return 0

    lax.fori_loop(0, o_ref.shape[0] // row_chunk, body, 0)


def _accumulate_matmul(o_ref, lhs, w_ref, col_chunk=1024):
    n = o_ref.shape[1]
    for c0 in range(0, n, col_chunk):
        cs = slice(c0, min(c0 + col_chunk, n))
        o_ref[:, cs] += jnp.dot(lhs, w_ref[:, cs], preferred_element_type=F32)


def _deepnorm_epilogue(o_ref, x_ref, gm_ref, lng_ref, lnb_ref, row_chunk=128):
    gate = 1.0 + gm_ref[...]
    lng = lng_ref[...]
    lnb = lnb_ref[...]

    def body(r, _):
        rs = pl.ds(pl.multiple_of(r * row_chunk, row_chunk), row_chunk)
        v = ALPHA * x_ref[rs, :] + gate * o_ref[rs, :]
        o_ref[rs, :] = _layer_norm(v, lng, lnb)
        return 0

    lax.fori_loop(0, o_ref.shape[0] // row_chunk, body, 0)


def _mod_kernel(ct_ref, w_ref, b_ref, o_ref, *, batch, row_chunk=512):
    d, tn = w_ref.shape

    def body(r, accs):
        rs = pl.ds(pl.multiple_of(r * row_chunk, row_chunk), row_chunk)
        c = ct_ref[rs, :]
        cs = c * jax.nn.sigmoid(c)
        w = w_ref[rs, :]
        return tuple(acc + jnp.sum(w * cs[:, b:b + 1], axis=0, keepdims=True) for b, acc in enumerate(accs))

    accs = lax.fori_loop(0, d // row_chunk, body, tuple(jnp.zeros((1, tn), F32) for _ in range(batch)))
    pad = jnp.zeros((o_ref.shape[0] - batch, tn), F32)
    o_ref[...] = jnp.concatenate(list(accs) + [pad], axis=0) + b_ref[...]


def _modulation(c_pad, w_mod, b_mod, batch, tn=1024):
    depth, d, n = w_mod.shape
    rows = c_pad.shape[0]
    c_t = c_pad.T
    return pl.pallas_call(
        functools.partial(_mod_kernel, batch=batch),
        grid=(depth, n // tn),
        in_specs=[
            pl.BlockSpec((d, rows), lambda l, j: (0, 0)),
            pl.BlockSpec((None, d, tn), lambda l, j: (l, 0, j)),
            pl.BlockSpec((None, 1, tn), lambda l, j: (l, 0, j)),
        ],
        out_specs=pl.BlockSpec((None, rows, tn), lambda l, j: (l, 0, j)),
        out_shape=jax.ShapeDtypeStruct((depth, rows, n), F32),
        compiler_params=_params(("arbitrary", "arbitrary")),
        name="modulation",
    )(c_t, w_mod, b_mod.reshape(depth, 1, n))


def _in_proj_kernel(x_ref, sc_ref, sh_ref, w_ref, za_ref, zb_ref, h_ref, *, na):
    j = pl.program_id(1)

    @pl.when(j == 0)
    def _():
        _modulate_into(h_ref, x_ref, sc_ref, sh_ref)

    r = jnp.dot(h_ref[...], w_ref[...], preferred_element_type=F32)

    @pl.when(j < na)
    def _():
        za_ref[...] = r.astype(za_ref.dtype)

    @pl.when(j >= na)
    def _():
        zb_ref[...] = r.astype(zb_ref.dtype)


def _block_order_map(order):
    runs, start = [], 0
    for j in range(1, len(order) + 1):
        if j == len(order) or order[j] != order[j - 1] + 1:
            runs.append((j, order[start] - start))
            start = j

    def block_of(j):
        out = j + runs[-1][1]
        for end, off in reversed(runs[:-1]):
            out = jnp.where(j < end, j + off, out)
        return out

    return block_of


def _in_proj(x, mod_l, sc_idx, sh_idx, w, order, na, rows_per_batch, tm=1024, tn=512):
    m, k = x.shape
    nb = len(order)
    block_of = _block_order_map(order)
    tpb = rows_per_batch // tm
    kern = functools.partial(_in_proj_kernel, na=na)
    return pl.pallas_call(
        kern,
        grid=(m // tm, nb),
        in_specs=[
            pl.BlockSpec((tm, k), lambda i, j: (i, 0)),
            pl.BlockSpec((None, 1, k), lambda i, j: (i // tpb, 0, sc_idx)),
            pl.BlockSpec((None, 1, k), lambda i, j: (i // tpb, 0, sh_idx)),
            pl.BlockSpec((k, tn), lambda i, j: (0, block_of(j))),
        ],
        out_specs=[
            pl.BlockSpec((tm, tn), lambda i, j: (i, jnp.minimum(j, na - 1))),
            pl.BlockSpec((tm, tn), lambda i, j: (i, jnp.maximum(j - na, 0))),
        ],
        out_shape=[jax.ShapeDtypeStruct((m, na * tn), BF16),
                   jax.ShapeDtypeStruct((m, (nb - na) * tn), F32)],
        scratch_shapes=[pltpu.VMEM((tm, k), BF16)],
        compiler_params=_params(("arbitrary", "arbitrary")),
        name="in_proj",
    )(x, mod_l, mod_l, w)


def _compress_kernel(kv_ref, pos_ref, w1_ref, b1_ref, w2_ref, o_ref, *, ncp):
    a0 = jnp.zeros((ncp, HEAD_DIM), F32)
    a1 = jnp.zeros((ncp, HEAD_DIM), F32)
    for r in range(CMP_STRIDE):
        xr = kv_ref[pl.ds(r, ncp, stride=CMP_STRIDE), :]
        lo = (xr + pos_ref[r:r + 1, :]).astype(BF16)
        hi = (xr + pos_ref[CMP_STRIDE + r:CMP_STRIDE + r + 1, :]).astype(BF16)
        a0 = a0 + jnp.dot(lo, w1_ref[r * HEAD_DIM:(r + 1) * HEAD_DIM, :], preferred_element_type=F32)
        a1 = a1 + jnp.dot(hi, w1_ref[(CMP_STRIDE + r) * HEAD_DIM:(CMP_STRIDE + r + 1) * HEAD_DIM, :],
                          preferred_element_type=F32)
    pre = a0 + pltpu.roll(a1, ncp - 1, 0) + b1_ref[...]
    hid = _gelu(pre).astype(BF16)
    o_ref[...] = jnp.dot(hid, w2_ref[...], preferred_element_type=F32).astype(o_ref.dtype)


def _compress(zb, cmp_pos, cmp_w1, cmp_b1, cmp_w2, batch, seq):
    ncp = seq // CMP_STRIDE
    kern = functools.partial(_compress_kernel, ncp=ncp)
    return pl.pallas_call(
        kern,
        grid=(2, batch, N_GROUP),
        in_specs=[
            pl.BlockSpec((seq, HEAD_DIM), lambda w, b, g: (b, w * N_GROUP + g)),
            pl.BlockSpec((None, CMP_LEN, HEAD_DIM), lambda w, b, g: (w, 0, 0)),
            pl.BlockSpec((None, CMP_LEN * HEAD_DIM, HEAD_DIM), lambda w, b, g: (w, 0, 0)),
            pl.BlockSpec((None, 1, HEAD_DIM), lambda w, b, g: (w, 0, 0)),
            pl.BlockSpec((None, HEAD_DIM, HEAD_DIM), lambda w, b, g: (w, 0, 0)),
        ],
        out_specs=pl.BlockSpec((None, None, None, ncp, HEAD_DIM), lambda w, b, g: (w, b, g, 0, 0)),
        out_shape=jax.ShapeDtypeStruct((2, batch, N_GROUP, ncp, HEAD_DIM), BF16),
        compiler_params=_params(("arbitrary", "arbitrary", "arbitrary")),
        name="nsa_compress",
    )(zb, cmp_pos, cmp_w1.astype(BF16), cmp_b1.reshape(2, 1, HEAD_DIM), cmp_w2.astype(BF16))


_NT = (((1,), (1,)), ((), ()))


def _attn_kernel(q_ref, ks_ref, vs_ref, kw_ref, vw_ref, ke_ref, kc_ref, vct_ref, cov_ref, gl_ref,
                 o_ref, qa_ref, *, tq, tk, ncp, nsel):
    half = min(HPG * tq, CHAIN_ROWS)
    nch = HPG * tq // half
    hpc = half // tq
    q0 = pl.program_id(2) * tq
    for h in range(HPG):
        qa_ref[h * tq:(h + 1) * tq, 0:HEAD_DIM] = q_ref[:, h * HEAD_DIM:(h + 1) * HEAD_DIM]

    ci = lax.broadcasted_iota(jnp.int32, (ncp, half), 0) * CMP_STRIDE + (CMP_LEN - 1)
    tc = lax.broadcasted_iota(jnp.int32, (ncp, half), 1) & (tq - 1)
    valid_c = ci - tc <= q0
    t_row = q0 + (lax.broadcasted_iota(jnp.int32, (1, half), 1) & (tq - 1))
    any_c = (t_row >= CMP_LEN - 1).astype(F32)
    o_ct, p_sum = [], None
    for c in range(nch):
        qc = qa_ref[c * half:(c + 1) * half, 0:HEAD_DIM]
        s_c = lax.dot_general(kc_ref[...], qc, _NT, preferred_element_type=F32)
        s_c = jnp.where(valid_c, s_c, NEG)
        m_c = jnp.max(s_c, axis=0, keepdims=True)
        p_c = jnp.exp2((s_c - m_c) * EXP2_SCALE)
        l_c = jnp.sum(p_c, axis=0, keepdims=True)
        p_c = p_c * (any_c / l_c)
        o_ct.append(jnp.dot(vct_ref[...], p_c.astype(BF16), preferred_element_type=F32))
        for h in range(hpc):
            ph = p_c[:, h * tq:(h + 1) * tq]
            p_sum = ph if p_sum is None else p_sum + ph

    a = q0 // tk
    ts = 2 * tk

    def rel(width):
        return ((lax.broadcasted_iota(jnp.int32, (half, width), 0) & (tq - 1))
                - lax.broadcasted_iota(jnp.int32, (half, width), 1))

    ones_col = jnp.where(lax.broadcasted_iota(jnp.int32, (ts, LANE), 1) == 0, 1.0, 0.0).astype(BF16)

    def with_ones(v):
        return jnp.concatenate([v, ones_col], axis=1)

    def online(carry, s, v_aug):
        m, acc = carry
        mn = jnp.maximum(m, jnp.max(s, axis=1, keepdims=True))
        al = jnp.exp2((m - mn) * EXP2_SCALE)
        p = jnp.exp2((s - mn) * EXP2_SCALE).astype(BF16)
        acc = al * acc + jnp.dot(p, v_aug, preferred_element_type=F32)
        return mn, acc

    def normalised(state, sl):
        acc = state[1]
        return acc[sl, 0:HEAD_DIM] * (1.0 / acc[sl, HEAD_DIM:HEAD_DIM + 1])

    init1 = (jnp.full((half, 1), NEG, F32), jnp.zeros((half, HEAD_DIM + LANE), F32))

    w0 = pl.multiple_of(jnp.maximum(a - 1, 0) * tk, tk)
    kw = kw_ref[pl.ds(w0, ts), :]
    vw = with_ones(vw_ref[pl.ds(w0, ts), :])
    rel_w = rel(ts)
    win = []
    for c in range(nch):
        s = lax.dot_general(qa_ref[c * half:(c + 1) * half, 0:HEAD_DIM], kw, _NT, preferred_element_type=F32)
        s = jnp.where(rel_w >= w0 - q0, jnp.where(rel_w < WINDOW + w0 - q0, s, NEG), NEG)
        win.append(online(init1, s, vw))

    imp = jnp.dot(cov_ref[...], p_sum.astype(BF16), preferred_element_type=F32).T
    jb = lax.broadcasted_iota(jnp.int32, (tq, SEL_PAD), 1)
    tl = q0 + lax.broadcasted_iota(jnp.int32, (tq, SEL_PAD), 0)
    cur = lax.shift_right_logical(tl, 6)
    forced = (jb == 0) | (jb == cur) | (jb == cur - 1)
    imp = jnp.where(forced, FORCE, jnp.where(jb * SEL_BLOCK <= tl, imp, NEG))
    if nsel < SEL_PAD:
        imp = jnp.where(jb < nsel, imp, -jnp.inf)
    selb = jnp.full((tq, SEL_PAD), NEG, F32)
    for _ in range(min(SEL_TOPK, nsel)):
        hit = jb == jnp.argmax(imp, axis=1, keepdims=True)
        selb = jnp.where(hit, 0.0, selb)
        imp = jnp.where(hit, -jnp.inf, imp)
    selb_b = selb.astype(BF16)
    for h in range(HPG):
        qa_ref[h * tq:(h + 1) * tq, HEAD_DIM:HEAD_DIM + SEL_PAD] = selb_b

    def sel_tile(kt, carry, causal):
        start = pl.multiple_of(kt * ts, ts)
        kaug = jnp.concatenate([ks_ref[pl.ds(start, ts), :], ke_ref[pl.ds(start, ts), :]], axis=1)
        v = with_ones(vs_ref[pl.ds(start, ts), :])
        out = []
        for c in range(nch):
            s = lax.dot_general(qa_ref[c * half:(c + 1) * half, :], kaug, _NT, preferred_element_type=F32)
            if causal:
                s = jnp.where(rel(ts) >= start - q0, s, NEG)
            out.append(online(carry[c], s, v))
        return tuple(out)

    n_full = a // 2
    carry = lax.fori_loop(0, n_full, lambda kt, cr: sel_tile(kt, cr, False), (init1,) * nch)
    sel = sel_tile(n_full, carry, True)

    gate = jax.nn.sigmoid(gl_ref[...])
    for h in range(HPG):
        c, hh = divmod(h, hpc)
        sl = slice(hh * tq, (hh + 1) * tq)
        o_c = o_ct[c][:, sl].T
        o_s = normalised(sel[c], sl)
        o_w = normalised(win[c], sl)
        out = (gate[:, 3 * h:3 * h + 1] * o_c + gate[:, 3 * h + 1:3 * h + 2] * o_s
               + gate[:, 3 * h + 2:3 * h + 3] * o_w)
        o_ref[:, h * HEAD_DIM:(h + 1) * HEAD_DIM] = out.astype(o_ref.dtype)


def _selection_constants(seq):
    ncp = seq // CMP_STRIDE
    nc = ncp - CMP_LEN // CMP_STRIDE + 1
    nsel = seq // SEL_BLOCK
    ci = np.arange(ncp)[None, :] * CMP_STRIDE
    sj = np.arange(SEL_PAD)[:, None] * SEL_BLOCK
    cover_t = ((ci < sj + SEL_BLOCK) & (ci + CMP_LEN > sj)
               & (np.arange(ncp)[None, :] < nc) & (np.arange(SEL_PAD)[:, None] < nsel))
    key_block = (np.arange(seq)[:, None] // SEL_BLOCK) == np.arange(SEL_PAD)[None, :]
    return jnp.asarray(cover_t, BF16), jnp.asarray(key_block, BF16)


def _attention(za, kc, vct, gl, batch, seq, tq=512, tk=512):
    assert tk == WINDOW and tk % tq == 0 and tq & (tq - 1) == 0 and seq % (2 * tk) == 0
    ncp = seq // CMP_STRIDE
    nsel = seq // SEL_BLOCK
    nq = seq // tq
    cover_t, key_block = _selection_constants(seq)
    qcols = ATTN_W // LANE
    kern = functools.partial(_attn_kernel, tq=tq, tk=tk, ncp=ncp, nsel=nsel)
    kv_spec = lambda off: pl.BlockSpec((seq, HEAD_DIM), lambda b, g, i: (b, qcols + off * N_GROUP + g))
    return pl.pallas_call(
        kern,
        grid=(batch, N_GROUP, nq),
        in_specs=[
            pl.BlockSpec((tq, HPG * HEAD_DIM), lambda b, g, i: (b * nq + i, g)),
            kv_spec(0), kv_spec(1), kv_spec(2), kv_spec(3),
            pl.BlockSpec((seq, SEL_PAD), lambda b, g, i: (0, 0)),
            pl.BlockSpec((None, None, ncp, HEAD_DIM), lambda b, g, i: (b, g, 0, 0)),
            pl.BlockSpec((None, None, HEAD_DIM, ncp), lambda b, g, i: (b, g, 0, 0)),
            pl.BlockSpec((SEL_PAD, ncp), lambda b, g, i: (0, 0)),
            pl.BlockSpec((None, tq, N_GATE), lambda b, g, i: (g, b * nq + i, 0)),
        ],
        out_specs=pl.BlockSpec((tq, HPG * HEAD_DIM), lambda b, g, i: (b * nq + i, g)),
        out_shape=jax.ShapeDtypeStruct((batch * seq, ATTN_W), BF16),
        scratch_shapes=[pltpu.VMEM((HPG * tq, HEAD_DIM + SEL_PAD), BF16)],
        compiler_params=_params(("arbitrary", "arbitrary", "arbitrary")),
        name="nsa_attention",
    )(za, za, za, za, za, key_block, kc, vct, cover_t, gl)


def _lru_kernel(zg_ref, zr_ref, cw_ref, cb_ref, wa_ref, ba_ref, wx_ref, bx_ref, lam_ref, o_ref,
                tail_ref, h_ref, *, tt):
    @pl.when(pl.program_id(1) == 0)
    def _():
        tail_ref[...] = jnp.zeros_like(tail_ref)
        h_ref[...] = jnp.zeros_like(h_ref)

    zr = zr_ref[...]
    width = zr.shape[1]
    prev = tail_ref[...]
    xr = cw_ref[LRU_CONV - 1:LRU_CONV, :] * zr + cb_ref[...]
    for k in range(1, LRU_CONV):
        xr = xr + cw_ref[LRU_CONV - 1 - k:LRU_CONV - k, :] * _shift_rows(zr, prev, k)
    tail_ref[...] = zr[tt - SUBLANE:tt]

    xb = xr.astype(BF16)
    bw = width // LRU_BLOCKS
    r_parts, i_parts = [], []
    for n in range(LRU_BLOCKS):
        xs = xb[:, n * bw:(n + 1) * bw]
        r_parts.append(jnp.dot(xs, wa_ref[n], preferred_element_type=F32))
        i_parts.append(jnp.dot(xs, wx_ref[n], preferred_element_type=F32))
    rg = jax.nn.sigmoid(jnp.concatenate(r_parts, axis=1) + ba_ref[...])
    ig = jax.nn.sigmoid(jnp.concatenate(i_parts, axis=1) + bx_ref[...])
    lam = lam_ref[...]
    softplus_neg = jnp.maximum(-lam, 0.0) + jnp.log1p(jnp.exp(-jnp.abs(lam)))
    log_a = (-LRU_C) * rg * softplus_neg
    a = jnp.exp(log_a)
    b = jnp.sqrt(1.0 - a * a) * (ig * xr)

    row_in_group = lax.broadcasted_iota(jnp.int32, (tt, width), 0) & (SUBLANE - 1)
    d = 1
    while d < SUBLANE:
        a_s = pltpu.roll(a, d, 0)
        b_s = pltpu.roll(b, d, 0)
        keep = row_in_group >= d
        b = jnp.where(keep, a * b_s + b, b)
        a = jnp.where(keep, a * a_s, a)
        d *= 2
    h_prev = h_ref[...]
    groups = []
    for j in range(tt // SUBLANE):
        rs = slice(j * SUBLANE, (j + 1) * SUBLANE)
        h_j = a[rs] * h_prev + b[rs]
        groups.append(h_j)
        h_prev = h_j[SUBLANE - 1:SUBLANE]
    h = jnp.concatenate(groups, axis=0)
    h_ref[...] = h_prev
    o_ref[...] = (_gelu(zg_ref[...]) * h).astype(o_ref.dtype)


def _lru(zb, conv_w, conv_b, wa, ba, wx, bx, lam, batch, seq, tt=256):
    nt = seq // tt
    cg, cr = ZB_G // LRU_W, ZB_R // LRU_W
    kern = functools.partial(_lru_kernel, tt=tt)
    vec = lambda: pl.BlockSpec((1, LRU_W), lambda b, t: (0, 0))
    mat = lambda: pl.BlockSpec((LRU_BLOCKS, LRU_W // LRU_BLOCKS, LRU_W // LRU_BLOCKS), lambda b, t: (0, 0, 0))
    return pl.pallas_call(
        kern,
        grid=(batch, nt),
        in_specs=[
            pl.BlockSpec((tt, LRU_W), lambda b, t: (b * nt + t, cg)),
            pl.BlockSpec((tt, LRU_W), lambda b, t: (b * nt + t, cr)),
            pl.BlockSpec((LRU_CONV, LRU_W), lambda b, t: (0, 0)),
            vec(), mat(), vec(), mat(), vec(), vec(),
        ],
        out_specs=pl.BlockSpec((tt, LRU_W), lambda b, t: (b * nt + t, 0)),
        out_shape=jax.ShapeDtypeStruct((batch * seq, LRU_W), BF16),
        scratch_shapes=[pltpu.VMEM((SUBLANE, LRU_W), F32), pltpu.VMEM((1, LRU_W), F32)],
        compiler_params=_params(("arbitrary", "arbitrary")),
        name="rg_lru",
    )(zb, zb, conv_w, conv_b.reshape(1, LRU_W), wa.astype(BF16), ba.reshape(1, LRU_W),
      wx.astype(BF16), bx.reshape(1, LRU_W), lam.reshape(1, LRU_W))


def _sgu_kernel(zu_ref, zv_ref, g_ref, b_ref, ws_ref, bst_ref, o_ref, *, nchunk):
    v = _layer_norm(_gelu(zv_ref[...]), g_ref[...], b_ref[...]).astype(BF16)
    u = _gelu(zu_ref[...])
    gw = GMLP_W // GMLP_GROUPS
    tril = (lax.broadcasted_iota(jnp.int32, (CHUNK, CHUNK), 0)
            >= lax.broadcasted_iota(jnp.int32, (CHUNK, CHUNK), 1))
    for gi in range(GMLP_GROUPS):
        cs = slice(gi * gw, (gi + 1) * gw)
        w = jnp.where(tril, ws_ref[gi], 0.0).astype(BF16)
        rhs = jnp.concatenate([v[c * CHUNK:(c + 1) * CHUNK, cs] for c in range(nchunk)], axis=1)
        y = jnp.dot(w, rhs, preferred_element_type=F32) + bst_ref[:, gi:gi + 1]
        for c in range(nchunk):
            rs = slice(c * CHUNK, (c + 1) * CHUNK)
            o_ref[rs, cs] = (u[rs, cs] * y[:, c * gw:(c + 1) * gw]).astype(o_ref.dtype)


def _sgu(zb, ln_g, ln_b, w_s, b_s, batch, seq, nchunk=4):
    tt = nchunk * CHUNK
    nt = batch * seq // tt
    cu, cv = ZB_U // GMLP_W, ZB_V // GMLP_W
    kern = functools.partial(_sgu_kernel, nchunk=nchunk)
    return pl.pallas_call(
        kern,
        grid=(nt,),
        in_specs=[
            pl.BlockSpec((tt, GMLP_W), lambda t: (t, cu)),
            pl.BlockSpec((tt, GMLP_W), lambda t: (t, cv)),
            pl.BlockSpec((1, GMLP_W), lambda t: (0, 0)),
            pl.BlockSpec((1, GMLP_W), lambda t: (0, 0)),
            pl.BlockSpec((GMLP_GROUPS, CHUNK, CHUNK), lambda t: (0, 0, 0)),
            pl.BlockSpec((CHUNK, GMLP_GROUPS), lambda t: (0, 0)),
        ],
        out_specs=pl.BlockSpec((tt, GMLP_W), lambda t: (t, 0)),
        out_shape=jax.ShapeDtypeStruct((batch * seq, GMLP_W), BF16),
        compiler_params=_params(("arbitrary",)),
        name="spatial_gating",
    )(zb, zb, ln_g.reshape(1, GMLP_W), ln_b.reshape(1, GMLP_W), w_s, b_s.T)


def _oproj_kernel(ya_ref, yl_ref, ys_ref, w_ref, x_ref, gm_ref, lng_ref, lnb_ref, o_ref, *, na):
    k = pl.program_id(1)

    @pl.when(k == 0)
    def _():
        _zero_rows(o_ref)

    lhs = jnp.where(k < na, ya_ref[...], jnp.where(k == na, yl_ref[...], ys_ref[...]))
    _accumulate_matmul(o_ref, lhs, w_ref)

    @pl.when(k == pl.num_programs(1) - 1)
    def _():
        _deepnorm_epilogue(o_ref, x_ref, gm_ref, lng_ref, lnb_ref)


def _oproj(y_attn, y_lru, y_sgu, w_all, layer, x, mod_l, gate_idx, ln_g, ln_b, rows_per_batch, tm=512):
    m = y_attn.shape[0]
    tk = y_lru.shape[1]
    assert y_sgu.shape[1] == tk and y_attn.shape[1] % tk == 0
    na = y_attn.shape[1] // tk
    _, kdim, d = w_all.shape
    assert kdim == (na + 2) * tk
    tpb = rows_per_batch // tm
    return pl.pallas_call(
        functools.partial(_oproj_kernel, na=na),
        grid=(m // tm, na + 2),
        in_specs=[
            pl.BlockSpec((tm, tk), lambda i, k: (i, jnp.minimum(k, na - 1))),
            pl.BlockSpec((tm, tk), lambda i, k: (i, 0)),
            pl.BlockSpec((tm, tk), lambda i, k: (i, 0)),
            pl.BlockSpec((None, tk, d), lambda i, k: (layer, k, 0)),
            pl.BlockSpec((tm, d), lambda i, k: (i, 0)),
            pl.BlockSpec((None, 1, d), lambda i, k: (i // tpb, 0, gate_idx)),
            pl.BlockSpec((1, d), lambda i, k: (0, 0)),
            pl.BlockSpec((1, d), lambda i, k: (0, 0)),
        ],
        out_specs=pl.BlockSpec((tm, d), lambda i, k: (i, 0)),
        out_shape=jax.ShapeDtypeStruct((m, d), F32),
        compiler_params=_params(("arbitrary", "arbitrary")),
        name="out_proj_ln",
    )(y_attn, y_lru, y_sgu, w_all, x, mod_l, ln_g.reshape(1, d), ln_b.reshape(1, d))


def _ffn_kernel(x_ref, mod_ref, wg_ref, wu_ref, conv_ref, wd_ref, wd_last_ref, ln_ref,
                o_ref, h_ref, tail_ref, act_ref, *, tm, tpb, nf, sc_idx, sh_idx, gate_idx):
    i = pl.program_id(0)
    f = pl.program_id(1)
    d = x_ref.shape[1]
    sc_ref, sh_ref, gm_ref = (mod_ref.at[:, c * d:(c + 1) * d] for c in (sc_idx, sh_idx, gate_idx))
    cw_ref, cb_ref = conv_ref.at[0:FFN_CONV], conv_ref.at[FFN_CONV:FFN_CONV + 1]
    lng_ref, lnb_ref = ln_ref.at[0:1], ln_ref.at[1:2]

    @pl.when(f == 0)
    def _():
        _modulate_into(h_ref, x_ref, sc_ref, sh_ref)
        _zero_rows(o_ref)
        act_ref[1] = jnp.zeros(act_ref.shape[1:], act_ref.dtype)

    @pl.when((f == 0) & (i == 0))
    def _():
        tail_ref[...] = jnp.zeros(tail_ref.shape, tail_ref.dtype)

    h = h_ref[...]
    g = jnp.dot(h, wg_ref[...], preferred_element_type=F32)
    u = jnp.dot(h, wu_ref[...], preferred_element_type=F32)
    _accumulate_matmul(o_ref, act_ref[(f + 1) % 2], wd_ref)

    prev = jnp.where(i % tpb == 0, 0.0, tail_ref[f])
    tail_ref[f] = g[tm - SUBLANE:tm]
    gc = cw_ref[FFN_CONV - 1:FFN_CONV, :] * g + cb_ref[...]
    for k in range(1, FFN_CONV):
        gc = gc + cw_ref[FFN_CONV - 1 - k:FFN_CONV - k, :] * _shift_rows(g, prev, k)
    act_ref[f % 2] = (gc * jax.nn.sigmoid(gc) * u).astype(BF16)

    @pl.when(f == nf - 1)
    def _():
        _accumulate_matmul(o_ref, act_ref[(nf - 1) % 2], wd_last_ref)
        _deepnorm_epilogue(o_ref, x_ref, gm_ref, lng_ref, lnb_ref)


def _ffn(x, mod_l, sc_idx, sh_idx, gate_idx, w_in, w_down, layer, conv_w, conv_b, ln_g, ln_b, rows_per_batch,
         tm=512, tf=256):
    m, d = x.shape
    dff = w_down.shape[1]
    nf = dff // tf
    tpb = rows_per_batch // tm
    kern = functools.partial(_ffn_kernel, tm=tm, tpb=tpb, nf=nf, sc_idx=sc_idx, sh_idx=sh_idx,
                             gate_idx=gate_idx)
    prv = lambda f: jnp.maximum(f - 1, 0)
    conv = jnp.concatenate([conv_w, conv_b.reshape(1, dff)], axis=0)
    ln = jnp.stack([ln_g, ln_b])
    return pl.pallas_call(
        kern,
        grid=(m // tm, nf),
        in_specs=[
            pl.BlockSpec((tm, d), lambda i, f: (i, 0)),
            pl.BlockSpec((None, 1, mod_l.shape[2]), lambda i, f: (i // tpb, 0, 0)),
            pl.BlockSpec((None, d, tf), lambda i, f: (layer, 0, f)),
            pl.BlockSpec((None, d, tf), lambda i, f: (layer, 0, nf + f)),
            pl.BlockSpec((FFN_CONV + 1, tf), lambda i, f: (0, f)),
            pl.BlockSpec((None, tf, d), lambda i, f: (layer, prv(f), 0)),
            pl.BlockSpec((None, tf, d), lambda i, f: (layer, nf - 1, 0), pipeline_mode=pl.Buffered(1)),
            pl.BlockSpec((2, d), lambda i, f: (0, 0)),
        ],
        out_specs=pl.BlockSpec((tm, d), lambda i, f: (i, 0)),
        out_shape=jax.ShapeDtypeStruct((m, d), F32),
        scratch_shapes=[pltpu.VMEM((tm, d), BF16), pltpu.VMEM((nf, SUBLANE, tf), F32),
                        pltpu.VMEM((2, tm, tf), BF16)],
        compiler_params=_params(("arbitrary", "arbitrary")),
        name="conv_ffn_ln",
    )(x, mod_l, w_in, w_in, conv, w_down, w_down, ln)


def _prepare_w_in(w, tn=512):
    a0 = ATTN_W + 6 * KV_W
    a1 = a0 + N_GATE * N_GROUP
    pad = jnp.zeros((w.shape[0], ZB_COLS - ZB_GL - N_GATE * N_GROUP), w.dtype)
    wn = jnp.concatenate([w[:, :a0], w[:, a1:], w[:, a0:a1], pad], axis=1).astype(BF16)
    blk = lambda lo, hi: tuple(range(lo // tn, hi // tn))
    qb = blk(0, ATTN_W)
    kcvc = blk(ATTN_W, ATTN_W + 2 * KV_W)
    kv4 = blk(ATTN_W + 2 * KV_W, a0)
    rest = blk(a0, wn.shape[1])
    return wn, qb + kv4 + kcvc + rest, len(qb + kv4)


def kernel(x, c, w_mod, b_mod, w_in, cmp_pos, cmp_w1, cmp_b1, cmp_w2, lru_conv_w, lru_conv_b, lru_wa,
           lru_ba, lru_wx, lru_bx, lru_lambda, sgu_ln_g, sgu_ln_b, sgu_w, sgu_b, w_o, ln1_g, ln1_b,
           ffn_w_in, ffn_conv_w, ffn_conv_b, ffn_w_down, ln2_g, ln2_b):
    batch, seq, d = x.shape
    depth = w_mod.shape[0]
    m = batch * seq
    x2 = x.reshape(m, d)
    c_pad = jnp.pad(c, ((0, SUBLANE - batch), (0, 0)))
    mod = _modulation(c_pad, w_mod, b_mod, batch).reshape(depth, SUBLANE, 1, 6 * d)
    w_o_b = w_o.astype(BF16)
    ffn_w_in_b = ffn_w_in.astype(BF16)
    ffn_w_down_b = ffn_w_down.astype(BF16)
    for l in range(depth):
        mod_l = mod[l]
        wn, order, na = _prepare_w_in(w_in[l])
        za, zb = _in_proj(x2, mod_l, 1, 0, wn, order, na, seq)
        cmp = _compress(zb, cmp_pos[l], cmp_w1[l], cmp_b1[l], cmp_w2[l], batch, seq)
        kc = cmp[0]
        vct = jnp.swapaxes(cmp[1], -1, -2)
        gl = zb[:, ZB_GL:ZB_GL + N_GATE * N_GROUP].reshape(m, N_GROUP, N_GATE).transpose(1, 0, 2)
        y_attn = _attention(za, kc, vct, gl, batch, seq)
        y_lru = _lru(zb, lru_conv_w[l], lru_conv_b[l], lru_wa[l], lru_ba[l], lru_wx[l], lru_bx[l],
                     lru_lambda[l], batch, seq)
        y_sgu = _sgu(zb, sgu_ln_g[l], sgu_ln_b[l], sgu_w[l], sgu_b[l], batch, seq)
        x2 = _oproj(y_attn, y_lru, y_sgu, w_o_b, l, x2, mod_l, 2, ln1_g[l], ln1_b[l], seq)
        x2 = _ffn(x2, mod_l, 4, 3, 5, ffn_w_in_b, ffn_w_down_b, l, ffn_conv_w[l], ffn_conv_b[l],
                  ln2_g[l], ln2_b[l], seq)
    return x2.reshape(batch, seq, d)
```

```python
import functools

import jax
import jax.numpy as jnp
import numpy as np
from jax import lax
from jax.experimental import pallas as pl
from jax.experimental.pallas import tpu as pltpu

F32 = jnp.float32
BF16 = jnp.bfloat16

DEPTH = 2
HEAD_DIM = 128
HPG = 4
N_GROUP = 4
ATTN_W = HPG * N_GROUP * HEAD_DIM
KV_W = N_GROUP * HEAD_DIM
LRU_W = 1024
LRU_BLOCKS = 8
LRU_CONV = 4
LRU_C = 8.0
GMLP_W = 1024
GMLP_GROUPS = 8
CHUNK = 128
CMP_LEN = 32
CMP_STRIDE = 16
SEL_BLOCK = 64
SEL_TOPK = 16
WINDOW = 512
FFN_CONV = 3
NEG = -1e30
FORCE = 1e4
LN_EPS = 1e-5
ALPHA = (2.0 * DEPTH) ** 0.25
SCALE = HEAD_DIM ** -0.5
EXP2_SCALE = SCALE * 1.4426950408889634
N_GATE = 3 * HPG
SEL_PAD = 128
CHAIN_ROWS = 512

V7X_VMEM_LIMIT_BYTES = 58 * 1024 * 1024
LANE = 128
SUBLANE = 8

ZB_KC, ZB_VC, ZB_G, ZB_R, ZB_U, ZB_V, ZB_GL = 0, 512, 1024, 2048, 3072, 4096, 5120
ZB_COLS = 5632


def _params(sem):
    return pltpu.CompilerParams(dimension_semantics=sem, vmem_limit_bytes=V7X_VMEM_LIMIT_BYTES)


def _gelu(x):
    return 0.5 * x * (1.0 + jnp.tanh(0.7978845608028654 * (x + 0.044715 * (x * x * x))))


def _layer_norm(v, g, b):
    mu = jnp.mean(v, axis=-1, keepdims=True)
    d = v - mu
    var = jnp.mean(d * d, axis=-1, keepdims=True)
    return d * lax.rsqrt(var + LN_EPS) * g + b


def _shift_rows(cur, prev_tail, k):
    r = pltpu.roll(cur, k, 0)
    rp = pltpu.roll(prev_tail, k, 0)
    row = lax.broadcasted_iota(jnp.int32, prev_tail.shape, 0)
    first = jnp.where(row < k, rp, r[0:SUBLANE])
    return jnp.concatenate([first, r[SUBLANE:]], axis=0)


def _modulate_into(h_ref, x_ref, sc_ref, sh_ref, row_chunk=128):
    scale = 1.0 + sc_ref[...]
    shift = sh_ref[...]

    def body(r, _):
        rs = pl.ds(pl.multiple_of(r * row_chunk, row_chunk), row_chunk)
        h_ref[rs, :] = (x_ref[rs, :] * scale + shift).astype(h_ref.dtype)
        return 0

    lax.fori_loop(0, h_ref.shape[0] // row_chunk, body, 0)


def _zero_rows(o_ref, row_chunk=128):
    def body(r, _):
        rs = pl.ds(pl.multiple_of(r * row_chunk, row_chunk), row_chunk)
        o_ref[rs, :] = jnp.zeros((row_chunk, o_ref.shape[1]), o_ref.dtype)
        return 0

    lax.fori_loop(0, o_ref.shape[0] // row_chunk, body, 0)


def _accumulate_matmul(o_ref, lhs, w_ref, col_chunk=1024):
    n = o_ref.shape[1]
    for c0 in range(0, n, col_chunk):
        cs = slice(c0, min(c0 + col_chunk, n))
        o_ref[:, cs] += jnp.dot(lhs, w_ref[:, cs], preferred_element_type=F32)


def _deepnorm_epilogue(o_ref, x_ref, gm_ref, lng_ref, lnb_ref, row_chunk=128):
    gate = 1.0 + gm_ref[...]
    lng = lng_ref[...]
    lnb = lnb_ref[...]

    def body(r, _):
        rs = pl.ds(pl.multiple_of(r * row_chunk, row_chunk), row_chunk)
        v = ALPHA * x_ref[rs, :] + gate * o_ref[rs, :]
        o_ref[rs, :] = _layer_norm(v, lng, lnb)
        return 0

    lax.fori_loop(0, o_ref.shape[0] // row_chunk, body, 0)


def _mod_kernel(ct_ref, w_ref, b_ref, o_ref, *, batch, col_chunk=4096):
    @pl.when(pl.program_id(1) == 0)
    def _():
        o_ref[...] = jnp.broadcast_to(b_ref[...], o_ref.shape)

    c = ct_ref[...]
    cs = c * jax.nn.sigmoid(c)
    n = w_ref.shape[1]
    for c0 in range(0, n, col_chunk):
        w = w_ref[:, c0:c0 + col_chunk]
        for b in range(batch):
            o_ref[b:b + 1, c0:c0 + col_chunk] += jnp.sum(w * cs[:, b:b + 1], axis=0, keepdims=True)


def _modulation(c_pad, w_mod, b_mod, batch, tk=128):
    depth, d, n = w_mod.shape
    rows = c_pad.shape[0]
    c_t = c_pad.T
    return pl.pallas_call(
        functools.partial(_mod_kernel, batch=batch),
        grid=(depth, d // tk),
        in_specs=[
            pl.BlockSpec((tk, rows), lambda l, k: (k, 0)),
            pl.BlockSpec((None, tk, n), lambda l, k: (l, k, 0)),
            pl.BlockSpec((None, 1, n), lambda l, k: (l, 0, 0)),
        ],
        out_specs=pl.BlockSpec((None, rows, n), lambda l, k: (l, 0, 0)),
        out_shape=jax.ShapeDtypeStruct((depth, rows, n), F32),
        compiler_params=_params(("arbitrary", "arbitrary")),
        name="modulation",
    )(c_t, w_mod, b_mod.reshape(depth, 1, n))


def _in_proj_kernel(x_ref, sc_ref, sh_ref, w_ref, za_ref, zb_ref, h_ref, *, na):
    j = pl.program_id(1)

    @pl.when(j == 0)
    def _():
        _modulate_into(h_ref, x_ref, sc_ref, sh_ref)

    r = jnp.dot(h_ref[...], w_ref[...], preferred_element_type=F32)

    @pl.when(j < na)
    def _():
        za_ref[...] = r.astype(za_ref.dtype)

    @pl.when(j >= na)
    def _():
        zb_ref[...] = r.astype(zb_ref.dtype)


def _block_order_map(order):
    runs, start = [], 0
    for j in range(1, len(order) + 1):
        if j == len(order) or order[j] != order[j - 1] + 1:
            runs.append((j, order[start] - start))
            start = j

    def block_of(j):
        out = j + runs[-1][1]
        for end, off in reversed(runs[:-1]):
            out = jnp.where(j < end, j + off, out)
        return out

    return block_of


def _in_proj(x, mod_l, sc_idx, sh_idx, w, order, na, rows_per_batch, tm=1024, tn=512):
    m, k = x.shape
    nb = len(order)
    block_of = _block_order_map(order)
    tpb = rows_per_batch // tm
    kern = functools.partial(_in_proj_kernel, na=na)
    return pl.pallas_call(
        kern,
        grid=(m // tm, nb),
        in_specs=[
            pl.BlockSpec((tm, k), lambda i, j: (i, 0)),
            pl.BlockSpec((None, 1, k), lambda i, j: (i // tpb, 0, sc_idx)),
            pl.BlockSpec((None, 1, k), lambda i, j: (i // tpb, 0, sh_idx)),
            pl.BlockSpec((k, tn), lambda i, j: (0, block_of(j))),
        ],
        out_specs=[
            pl.BlockSpec((tm, tn), lambda i, j: (i, jnp.minimum(j, na - 1))),
            pl.BlockSpec((tm, tn), lambda i, j: (i, jnp.maximum(j - na, 0))),
        ],
        out_shape=[jax.ShapeDtypeStruct((m, na * tn), BF16),
                   jax.ShapeDtypeStruct((m, (nb - na) * tn), F32)],
        scratch_shapes=[pltpu.VMEM((tm, k), BF16)],
        compiler_params=_params(("arbitrary", "arbitrary")),
        name="in_proj",
    )(x, mod_l, mod_l, w)


def _compress_kernel(kv_ref, pos_ref, w1_ref, b1_ref, w2_ref, o_ref, *, ncp):
    a0 = jnp.zeros((ncp, HEAD_DIM), F32)
    a1 = jnp.zeros((ncp, HEAD_DIM), F32)
    for r in range(CMP_STRIDE):
        xr = kv_ref[pl.ds(r, ncp, stride=CMP_STRIDE), :]
        lo = (xr + pos_ref[r:r + 1, :]).astype(BF16)
        hi = (xr + pos_ref[CMP_STRIDE + r:CMP_STRIDE + r + 1, :]).astype(BF16)
        a0 = a0 + jnp.dot(lo, w1_ref[r * HEAD_DIM:(r + 1) * HEAD_DIM, :], preferred_element_type=F32)
        a1 = a1 + jnp.dot(hi, w1_ref[(CMP_STRIDE + r) * HEAD_DIM:(CMP_STRIDE + r + 1) * HEAD_DIM, :],
                          preferred_element_type=F32)
    pre = a0 + pltpu.roll(a1, ncp - 1, 0) + b1_ref[...]
    hid = _gelu(pre).astype(BF16)
    o_ref[...] = jnp.dot(hid, w2_ref[...], preferred_element_type=F32).astype(o_ref.dtype)


def _compress(zb, cmp_pos, cmp_w1, cmp_b1, cmp_w2, batch, seq):
    ncp = seq // CMP_STRIDE
    kern = functools.partial(_compress_kernel, ncp=ncp)
    return pl.pallas_call(
        kern,
        grid=(2, batch, N_GROUP),
        in_specs=[
            pl.BlockSpec((seq, HEAD_DIM), lambda w, b, g: (b, w * N_GROUP + g)),
            pl.BlockSpec((None, CMP_LEN, HEAD_DIM), lambda w, b, g: (w, 0, 0)),
            pl.BlockSpec((None, CMP_LEN * HEAD_DIM, HEAD_DIM), lambda w, b, g: (w, 0, 0)),
            pl.BlockSpec((None, 1, HEAD_DIM), lambda w, b, g: (w, 0, 0)),
            pl.BlockSpec((None, HEAD_DIM, HEAD_DIM), lambda w, b, g: (w, 0, 0)),
        ],
        out_specs=pl.BlockSpec((None, None, None, ncp, HEAD_DIM), lambda w, b, g: (w, b, g, 0, 0)),
        out_shape=jax.ShapeDtypeStruct((2, batch, N_GROUP, ncp, HEAD_DIM), BF16),
        compiler_params=_params(("arbitrary", "arbitrary", "arbitrary")),
        name="nsa_compress",
    )(zb, cmp_pos, cmp_w1.astype(BF16), cmp_b1.reshape(2, 1, HEAD_DIM), cmp_w2.astype(BF16))


_NT = (((1,), (1,)), ((), ()))


def _attn_kernel(q_ref, ks_ref, vs_ref, kw_ref, vw_ref, ke_ref, kc_ref, vct_ref, cov_ref, gl_ref,
                 o_ref, qa_ref, *, tq, tk, ncp, nsel):
    half = min(HPG * tq, CHAIN_ROWS)
    nch = HPG * tq // half
    hpc = half // tq
    q0 = pl.program_id(2) * tq
    for h in range(HPG):
        qa_ref[h * tq:(h + 1) * tq, 0:HEAD_DIM] = q_ref[:, h * HEAD_DIM:(h + 1) * HEAD_DIM]

    ci = lax.broadcasted_iota(jnp.int32, (ncp, half), 0) * CMP_STRIDE + (CMP_LEN - 1)
    tc = lax.broadcasted_iota(jnp.int32, (ncp, half), 1) & (tq - 1)
    valid_c = ci - tc <= q0
    t_row = q0 + (lax.broadcasted_iota(jnp.int32, (1, half), 1) & (tq - 1))
    any_c = (t_row >= CMP_LEN - 1).astype(F32)
    o_ct, p_sum = [], None
    for c in range(nch):
        qc = qa_ref[c * half:(c + 1) * half, 0:HEAD_DIM]
        s_c = lax.dot_general(kc_ref[...], qc, _NT, preferred_element_type=F32)
        s_c = jnp.where(valid_c, s_c, NEG)
        m_c = jnp.max(s_c, axis=0, keepdims=True)
        p_c = jnp.exp2((s_c - m_c) * EXP2_SCALE)
        l_c = jnp.sum(p_c, axis=0, keepdims=True)
        p_c = p_c * (any_c / l_c)
        o_ct.append(jnp.dot(vct_ref[...], p_c.astype(BF16), preferred_element_type=F32))
        for h in range(hpc):
            ph = p_c[:, h * tq:(h + 1) * tq]
            p_sum = ph if p_sum is None else p_sum + ph

    a = q0 // tk
    ts = 2 * tk

    def rel(width):
        return ((lax.broadcasted_iota(jnp.int32, (half, width), 0) & (tq - 1))
                - lax.broadcasted_iota(jnp.int32, (half, width), 1))

    ones_col = jnp.where(lax.broadcasted_iota(jnp.int32, (ts, LANE), 1) == 0, 1.0, 0.0).astype(BF16)

    def with_ones(v):
        return jnp.concatenate([v, ones_col], axis=1)

    def online(carry, s, v_aug):
        m, acc = carry
        mn = jnp.maximum(m, jnp.max(s, axis=1, keepdims=True))
        al = jnp.exp2((m - mn) * EXP2_SCALE)
        p = jnp.exp2((s - mn) * EXP2_SCALE).astype(BF16)
        acc = al * acc + jnp.dot(p, v_aug, preferred_element_type=F32)
        return mn, acc

    def normalised(state, sl):
        acc = state[1]
        return acc[sl, 0:HEAD_DIM] * (1.0 / acc[sl, HEAD_DIM:HEAD_DIM + 1])

    init1 = (jnp.full((half, 1), NEG, F32), jnp.zeros((half, HEAD_DIM + LANE), F32))

    w0 = pl.multiple_of(jnp.maximum(a - 1, 0) * tk, tk)
    kw = kw_ref[pl.ds(w0, ts), :]
    vw = with_ones(vw_ref[pl.ds(w0, ts), :])
    rel_w = rel(ts)
    win = []
    for c in range(nch):
        s = lax.dot_general(qa_ref[c * half:(c + 1) * half, 0:HEAD_DIM], kw, _NT, preferred_element_type=F32)
        s = jnp.where(rel_w >= w0 - q0, jnp.where(rel_w < WINDOW + w0 - q0, s, NEG), NEG)
        win.append(online(init1, s, vw))

    imp = jnp.dot(cov_ref[...], p_sum.astype(BF16), preferred_element_type=F32).T
    jb = lax.broadcasted_iota(jnp.int32, (tq, SEL_PAD), 1)
    tl = q0 + lax.broadcasted_iota(jnp.int32, (tq, SEL_PAD), 0)
    cur = lax.shift_right_logical(tl, 6)
    forced = (jb == 0) | (jb == cur) | (jb == cur - 1)
    imp = jnp.where(forced, FORCE, jnp.where(jb * SEL_BLOCK <= tl, imp, NEG))
    if nsel < SEL_PAD:
        imp = jnp.where(jb < nsel, imp, -jnp.inf)
    selb = jnp.full((tq, SEL_PAD), NEG, F32)
    for _ in range(min(SEL_TOPK, nsel)):
        hit = jb == jnp.argmax(imp, axis=1, keepdims=True)
        selb = jnp.where(hit, 0.0, selb)
        imp = jnp.where(hit, -jnp.inf, imp)
    selb_b = selb.astype(BF16)
    for h in range(HPG):
        qa_ref[h * tq:(h + 1) * tq, HEAD_DIM:HEAD_DIM + SEL_PAD] = selb_b

    def sel_tile(kt, carry, causal):
        start = pl.multiple_of(kt * ts, ts)
        kaug = jnp.concatenate([ks_ref[pl.ds(start, ts), :], ke_ref[pl.ds(start, ts), :]], axis=1)
        v = with_ones(vs_ref[pl.ds(start, ts), :])
        out = []
        for c in range(nch):
            s = lax.dot_general(qa_ref[c * half:(c + 1) * half, :], kaug, _NT, preferred_element_type=F32)
            if causal:
                s = jnp.where(rel(ts) >= start - q0, s, NEG)
            out.append(online(carry[c], s, v))
        return tuple(out)

    n_full = a // 2
    carry = lax.fori_loop(0, n_full, lambda kt, cr: sel_tile(kt, cr, False), (init1,) * nch)
    sel = sel_tile(n_full, carry, True)

    gate = jax.nn.sigmoid(gl_ref[...])
    for h in range(HPG):
        c, hh = divmod(h, hpc)
        sl = slice(hh * tq, (hh + 1) * tq)
        o_c = o_ct[c][:, sl].T
        o_s = normalised(sel[c], sl)
        o_w = normalised(win[c], sl)
        out = (gate[:, 3 * h:3 * h + 1] * o_c + gate[:, 3 * h + 1:3 * h + 2] * o_s
               + gate[:, 3 * h + 2:3 * h + 3] * o_w)
        o_ref[:, h * HEAD_DIM:(h + 1) * HEAD_DIM] = out.astype(o_ref.dtype)


def _selection_constants(seq):
    ncp = seq // CMP_STRIDE
    nc = ncp - CMP_LEN // CMP_STRIDE + 1
    nsel = seq // SEL_BLOCK
    ci = np.arange(ncp)[None, :] * CMP_STRIDE
    sj = np.arange(SEL_PAD)[:, None] * SEL_BLOCK
    cover_t = ((ci < sj + SEL_BLOCK) & (ci + CMP_LEN > sj)
               & (np.arange(ncp)[None, :] < nc) & (np.arange(SEL_PAD)[:, None] < nsel))
    key_block = (np.arange(seq)[:, None] // SEL_BLOCK) == np.arange(SEL_PAD)[None, :]
    return jnp.asarray(cover_t, BF16), jnp.asarray(key_block, BF16)


def _attention(za, kc, vct, gl, batch, seq, tq=512, tk=512):
    assert tk == WINDOW and tk % tq == 0 and tq & (tq - 1) == 0 and seq % (2 * tk) == 0
    ncp = seq // CMP_STRIDE
    nsel = seq // SEL_BLOCK
    nq = seq // tq
    cover_t, key_block = _selection_constants(seq)
    qcols = ATTN_W // LANE
    kern = functools.partial(_attn_kernel, tq=tq, tk=tk, ncp=ncp, nsel=nsel)
    kv_spec = lambda off: pl.BlockSpec((seq, HEAD_DIM), lambda b, g, i: (b, qcols + off * N_GROUP + g))
    return pl.pallas_call(
        kern,
        grid=(batch, N_GROUP, nq),
        in_specs=[
            pl.BlockSpec((tq, HPG * HEAD_DIM), lambda b, g, i: (b * nq + i, g)),
            kv_spec(0), kv_spec(1), kv_spec(2), kv_spec(3),
            pl.BlockSpec((seq, SEL_PAD), lambda b, g, i: (0, 0)),
            pl.BlockSpec((None, None, ncp, HEAD_DIM), lambda b, g, i: (b, g, 0, 0)),
            pl.BlockSpec((None, None, HEAD_DIM, ncp), lambda b, g, i: (b, g, 0, 0)),
            pl.BlockSpec((SEL_PAD, ncp), lambda b, g, i: (0, 0)),
            pl.BlockSpec((None, tq, N_GATE), lambda b, g, i: (g, b * nq + i, 0)),
        ],
        out_specs=pl.BlockSpec((tq, HPG * HEAD_DIM), lambda b, g, i: (b * nq + i, g)),
        out_shape=jax.ShapeDtypeStruct((batch * seq, ATTN_W), BF16),
        scratch_shapes=[pltpu.VMEM((HPG * tq, HEAD_DIM + SEL_PAD), BF16)],
        compiler_params=_params(("arbitrary", "arbitrary", "arbitrary")),
        name="nsa_attention",
    )(za, za, za, za, za, key_block, kc, vct, cover_t, gl)


def _lru_kernel(zg_ref, zr_ref, cw_ref, cb_ref, wa_ref, ba_ref, wx_ref, bx_ref, lam_ref, o_ref,
                tail_ref, h_ref, *, tt):
    @pl.when(pl.program_id(1) == 0)
    def _():
        tail_ref[...] = jnp.zeros_like(tail_ref)
        h_ref[...] = jnp.zeros_like(h_ref)

    zr = zr_ref[...]
    width = zr.shape[1]
    prev = tail_ref[...]
    xr = cw_ref[LRU_CONV - 1:LRU_CONV, :] * zr + cb_ref[...]
    for k in range(1, LRU_CONV):
        xr = xr + cw_ref[LRU_CONV - 1 - k:LRU_CONV - k, :] * _shift_rows(zr, prev, k)
    tail_ref[...] = zr[tt - SUBLANE:tt]

    xb = xr.astype(BF16)
    bw = width // LRU_BLOCKS
    r_parts, i_parts = [], []
    for n in range(LRU_BLOCKS):
        xs = xb[:, n * bw:(n + 1) * bw]
        r_parts.append(jnp.dot(xs, wa_ref[n], preferred_element_type=F32))
        i_parts.append(jnp.dot(xs, wx_ref[n], preferred_element_type=F32))
    rg = jax.nn.sigmoid(jnp.concatenate(r_parts, axis=1) + ba_ref[...])
    ig = jax.nn.sigmoid(jnp.concatenate(i_parts, axis=1) + bx_ref[...])
    lam = lam_ref[...]
    softplus_neg = jnp.maximum(-lam, 0.0) + jnp.log1p(jnp.exp(-jnp.abs(lam)))
    log_a = (-LRU_C) * rg * softplus_neg
    a = jnp.exp(log_a)
    b = jnp.sqrt(1.0 - a * a) * (ig * xr)

    row_in_group = lax.broadcasted_iota(jnp.int32, (tt, width), 0) & (SUBLANE - 1)
    d = 1
    while d < SUBLANE:
        a_s = pltpu.roll(a, d, 0)
        b_s = pltpu.roll(b, d, 0)
        keep = row_in_group >= d
        b = jnp.where(keep, a * b_s + b, b)
        a = jnp.where(keep, a * a_s, a)
        d *= 2
    h_prev = h_ref[...]
    groups = []
    for j in range(tt // SUBLANE):
        rs = slice(j * SUBLANE, (j + 1) * SUBLANE)
        h_j = a[rs] * h_prev + b[rs]
        groups.append(h_j)
        h_prev = h_j[SUBLANE - 1:SUBLANE]
    h = jnp.concatenate(groups, axis=0)
    h_ref[...] = h_prev
    o_ref[...] = (_gelu(zg_ref[...]) * h).astype(o_ref.dtype)


def _lru(zb, conv_w, conv_b, wa, ba, wx, bx, lam, batch, seq, tt=256):
    nt = seq // tt
    cg, cr = ZB_G // LRU_W, ZB_R // LRU_W
    kern = functools.partial(_lru_kernel, tt=tt)
    vec = lambda: pl.BlockSpec((1, LRU_W), lambda b, t: (0, 0))
    mat = lambda: pl.BlockSpec((LRU_BLOCKS, LRU_W // LRU_BLOCKS, LRU_W // LRU_BLOCKS), lambda b, t: (0, 0, 0))
    return pl.pallas_call(
        kern,
        grid=(batch, nt),
        in_specs=[
            pl.BlockSpec((tt, LRU_W), lambda b, t: (b * nt + t, cg)),
            pl.BlockSpec((tt, LRU_W), lambda b, t: (b * nt + t, cr)),
            pl.BlockSpec((LRU_CONV, LRU_W), lambda b, t: (0, 0)),
            vec(), mat(), vec(), mat(), vec(), vec(),
        ],
        out_specs=pl.BlockSpec((tt, LRU_W), lambda b, t: (b * nt + t, 0)),
        out_shape=jax.ShapeDtypeStruct((batch * seq, LRU_W), BF16),
        scratch_shapes=[pltpu.VMEM((SUBLANE, LRU_W), F32), pltpu.VMEM((1, LRU_W), F32)],
        compiler_params=_params(("arbitrary", "arbitrary")),
        name="rg_lru",
    )(zb, zb, conv_w, conv_b.reshape(1, LRU_W), wa.astype(BF16), ba.reshape(1, LRU_W),
      wx.astype(BF16), bx.reshape(1, LRU_W), lam.reshape(1, LRU_W))


def _sgu_kernel(zu_ref, zv_ref, g_ref, b_ref, ws_ref, bst_ref, o_ref, *, nchunk):
    v = _layer_norm(_gelu(zv_ref[...]), g_ref[...], b_ref[...]).astype(BF16)
    u = _gelu(zu_ref[...])
    gw = GMLP_W // GMLP_GROUPS
    tril = (lax.broadcasted_iota(jnp.int32, (CHUNK, CHUNK), 0)
            >= lax.broadcasted_iota(jnp.int32, (CHUNK, CHUNK), 1))
    for gi in range(GMLP_GROUPS):
        cs = slice(gi * gw, (gi + 1) * gw)
        w = jnp.where(tril, ws_ref[gi], 0.0).astype(BF16)
        rhs = jnp.concatenate([v[c * CHUNK:(c + 1) * CHUNK, cs] for c in range(nchunk)], axis=1)
        y = jnp.dot(w, rhs, preferred_element_type=F32) + bst_ref[:, gi:gi + 1]
        for c in range(nchunk):
            rs = slice(c * CHUNK, (c + 1) * CHUNK)
            o_ref[rs, cs] = (u[rs, cs] * y[:, c * gw:(c + 1) * gw]).astype(o_ref.dtype)


def _sgu(zb, ln_g, ln_b, w_s, b_s, batch, seq, nchunk=4):
    tt = nchunk * CHUNK
    nt = batch * seq // tt
    cu, cv = ZB_U // GMLP_W, ZB_V // GMLP_W
    kern = functools.partial(_sgu_kernel, nchunk=nchunk)
    return pl.pallas_call(
        kern,
        grid=(nt,),
        in_specs=[
            pl.BlockSpec((tt, GMLP_W), lambda t: (t, cu)),
            pl.BlockSpec((tt, GMLP_W), lambda t: (t, cv)),
            pl.BlockSpec((1, GMLP_W), lambda t: (0, 0)),
            pl.BlockSpec((1, GMLP_W), lambda t: (0, 0)),
            pl.BlockSpec((GMLP_GROUPS, CHUNK, CHUNK), lambda t: (0, 0, 0)),
            pl.BlockSpec((CHUNK, GMLP_GROUPS), lambda t: (0, 0)),
        ],
        out_specs=pl.BlockSpec((tt, GMLP_W), lambda t: (t, 0)),
        out_shape=jax.ShapeDtypeStruct((batch * seq, GMLP_W), BF16),
        compiler_params=_params(("arbitrary",)),
        name="spatial_gating",
    )(zb, zb, ln_g.reshape(1, GMLP_W), ln_b.reshape(1, GMLP_W), w_s, b_s.T)


def _oproj_kernel(ya_ref, yl_ref, ys_ref, w_ref, x_ref, gm_ref, lng_ref, lnb_ref, o_ref, *, na):
    k = pl.program_id(1)

    @pl.when(k == 0)
    def _():
        _zero_rows(o_ref)

    lhs = jnp.where(k < na, ya_ref[...], jnp.where(k == na, yl_ref[...], ys_ref[...]))
    _accumulate_matmul(o_ref, lhs, w_ref)

    @pl.when(k == pl.num_programs(1) - 1)
    def _():
        _deepnorm_epilogue(o_ref, x_ref, gm_ref, lng_ref, lnb_ref)


def _oproj(y_attn, y_lru, y_sgu, w_all, layer, x, mod_l, gate_idx, ln_g, ln_b, rows_per_batch, tm=512):
    m = y_attn.shape[0]
    tk = y_lru.shape[1]
    assert y_sgu.shape[1] == tk and y_attn.shape[1] % tk == 0
    na = y_attn.shape[1] // tk
    _, kdim, d = w_all.shape
    assert kdim == (na + 2) * tk
    tpb = rows_per_batch // tm
    return pl.pallas_call(
        functools.partial(_oproj_kernel, na=na),
        grid=(m // tm, na + 2),
        in_specs=[
            pl.BlockSpec((tm, tk), lambda i, k: (i, jnp.minimum(k, na - 1))),
            pl.BlockSpec((tm, tk), lambda i, k: (i, 0)),
            pl.BlockSpec((tm, tk), lambda i, k: (i, 0)),
            pl.BlockSpec((None, tk, d), lambda i, k: (layer, k, 0)),
            pl.BlockSpec((tm, d), lambda i, k: (i, 0)),
            pl.BlockSpec((None, 1, d), lambda i, k: (i // tpb, 0, gate_idx)),
            pl.BlockSpec((1, d), lambda i, k: (0, 0)),
            pl.BlockSpec((1, d), lambda i, k: (0, 0)),
        ],
        out_specs=pl.BlockSpec((tm, d), lambda i, k: (i, 0)),
        out_shape=jax.ShapeDtypeStruct((m, d), F32),
        compiler_params=_params(("arbitrary", "arbitrary")),
        name="out_proj_ln",
    )(y_attn, y_lru, y_sgu, w_all, x, mod_l, ln_g.reshape(1, d), ln_b.reshape(1, d))


def _ffn_kernel(x_ref, mod_ref, wg_ref, wu_ref, conv_ref, wd_ref, wd_last_ref, ln_ref,
                o_ref, h_ref, tail_ref, act_ref, *, tm, tpb, nf, sc_idx, sh_idx, gate_idx):
    i = pl.program_id(0)
    f = pl.program_id(1)
    d = x_ref.shape[1]
    sc_ref, sh_ref, gm_ref = (mod_ref.at[:, c * d:(c + 1) * d] for c in (sc_idx, sh_idx, gate_idx))
    cw_ref, cb_ref = conv_ref.at[0:FFN_CONV], conv_ref.at[FFN_CONV:FFN_CONV + 1]
    lng_ref, lnb_ref = ln_ref.at[0:1], ln_ref.at[1:2]

    @pl.when(f == 0)
    def _():
        _modulate_into(h_ref, x_ref, sc_ref, sh_ref)
        _zero_rows(o_ref)
        act_ref[1] = jnp.zeros(act_ref.shape[1:], act_ref.dtype)

    @pl.when((f == 0) & (i == 0))
    def _():
        tail_ref[...] = jnp.zeros(tail_ref.shape, tail_ref.dtype)

    h = h_ref[...]
    g = jnp.dot(h, wg_ref[...], preferred_element_type=F32)
    u = jnp.dot(h, wu_ref[...], preferred_element_type=F32)
    _accumulate_matmul(o_ref, act_ref[(f + 1) % 2], wd_ref)

    prev = jnp.where(i % tpb == 0, 0.0, tail_ref[f])
    tail_ref[f] = g[tm - SUBLANE:tm]
    gc = cw_ref[FFN_CONV - 1:FFN_CONV, :] * g + cb_ref[...]
    for k in range(1, FFN_CONV):
        gc = gc + cw_ref[FFN_CONV - 1 - k:FFN_CONV - k, :] * _shift_rows(g, prev, k)
    act_ref[f % 2] = (gc * jax.nn.sigmoid(gc) * u).astype(BF16)

    @pl.when(f == nf - 1)
    def _():
        _accumulate_matmul(o_ref, act_ref[(nf - 1) % 2], wd_last_ref)
        _deepnorm_epilogue(o_ref, x_ref, gm_ref, lng_ref, lnb_ref)


def _ffn(x, mod_l, sc_idx, sh_idx, gate_idx, w_in, w_down, layer, conv_w, conv_b, ln_g, ln_b, rows_per_batch,
         tm=512, tf=256):
    m, d = x.shape
    dff = w_down.shape[1]
    nf = dff // tf
    tpb = rows_per_batch // tm
    kern = functools.partial(_ffn_kernel, tm=tm, tpb=tpb, nf=nf, sc_idx=sc_idx, sh_idx=sh_idx,
                             gate_idx=gate_idx)
    prv = lambda f: jnp.maximum(f - 1, 0)
    conv = jnp.concatenate([conv_w, conv_b.reshape(1, dff)], axis=0)
    ln = jnp.stack([ln_g, ln_b])
    return pl.pallas_call(
        kern,
        grid=(m // tm, nf),
        in_specs=[
            pl.BlockSpec((tm, d), lambda i, f: (i, 0)),
            pl.BlockSpec((None, 1, mod_l.shape[2]), lambda i, f: (i // tpb, 0, 0)),
            pl.BlockSpec((None, d, tf), lambda i, f: (layer, 0, f)),
            pl.BlockSpec((None, d, tf), lambda i, f: (layer, 0, nf + f)),
            pl.BlockSpec((FFN_CONV + 1, tf), lambda i, f: (0, f)),
            pl.BlockSpec((None, tf, d), lambda i, f: (layer, prv(f), 0)),
            pl.BlockSpec((None, tf, d), lambda i, f: (layer, nf - 1, 0), pipeline_mode=pl.Buffered(1)),
            pl.BlockSpec((2, d), lambda i, f: (0, 0)),
        ],
        out_specs=pl.BlockSpec((tm, d), lambda i, f: (i, 0)),
        out_shape=jax.ShapeDtypeStruct((m, d), F32),
        scratch_shapes=[pltpu.VMEM((tm, d), BF16), pltpu.VMEM((nf, SUBLANE, tf), F32),
                        pltpu.VMEM((2, tm, tf), BF16)],
        compiler_params=_params(("arbitrary", "arbitrary")),
        name="conv_ffn_ln",
    )(x, mod_l, w_in, w_in, conv, w_down, w_down, ln)


def _prepare_w_in(w, tn=512):
    a0 = ATTN_W + 6 * KV_W
    a1 = a0 + N_GATE * N_GROUP
    pad = jnp.zeros((w.shape[0], ZB_COLS - ZB_GL - N_GATE * N_GROUP), w.dtype)
    wn = jnp.concatenate([w[:, :a0], w[:, a1:], w[:, a0:a1], pad], axis=1).astype(BF16)
    blk = lambda lo, hi: tuple(range(lo // tn, hi // tn))
    qb = blk(0, ATTN_W)
    kcvc = blk(ATTN_W, ATTN_W + 2 * KV_W)
    kv4 = blk(ATTN_W + 2 * KV_W, a0)
    rest = blk(a0, wn.shape[1])
    return wn, qb + kv4 + kcvc + rest, len(qb + kv4)


def kernel(x, c, w_mod, b_mod, w_in, cmp_pos, cmp_w1, cmp_b1, cmp_w2, lru_conv_w, lru_conv_b, lru_wa,
           lru_ba, lru_wx, lru_bx, lru_lambda, sgu_ln_g, sgu_ln_b, sgu_w, sgu_b, w_o, ln1_g, ln1_b,
           ffn_w_in, ffn_conv_w, ffn_conv_b, ffn_w_down, ln2_g, ln2_b):
    batch, seq, d = x.shape
    depth = w_mod.shape[0]
    m = batch * seq
    x2 = x.reshape(m, d)
    c_pad = jnp.pad(c, ((0, SUBLANE - batch), (0, 0)))
    mod = _modulation(c_pad, w_mod, b_mod, batch).reshape(depth, SUBLANE, 1, 6 * d)
    w_o_b = w_o.astype(BF16)
    ffn_w_in_b = ffn_w_in.astype(BF16)
    ffn_w_down_b = ffn_w_down.astype(BF16)
    for l in range(depth):
        mod_l = mod[l]
        wn, order, na = _prepare_w_in(w_in[l])
        za, zb = _in_proj(x2, mod_l, 1, 0, wn, order, na, seq)
        cmp = _compress(zb, cmp_pos[l], cmp_w1[l], cmp_b1[l], cmp_w2[l], batch, seq)
        kc = cmp[0]
        vct = jnp.swapaxes(cmp[1], -1, -2)
        gl = zb[:, ZB_GL:ZB_GL + N_GATE * N_GROUP].reshape(m, N_GROUP, N_GATE).transpose(1, 0, 2)
        y_attn = _attention(za, kc, vct, gl, batch, seq)
        y_lru = _lru(zb, lru_conv_w[l], lru_conv_b[l], lru_wa[l], lru_ba[l], lru_wx[l], lru_bx[l],
                     lru_lambda[l], batch, seq)
        y_sgu = _sgu(zb, sgu_ln_g[l], sgu_ln_b[l], sgu_w[l], sgu_b[l], batch, seq)
        x2 = _oproj(y_attn, y_lru, y_sgu, w_o_b, l, x2, mod_l, 2, ln1_g[l], ln1_b[l], seq)
        x2 = _ffn(x2, mod_l, 4, 3, 5, ffn_w_in_b, ffn_w_down_b, l, ffn_conv_w[l], ffn_conv_b[l],
                  ln2_g[l], ln2_b[l], seq)
    return x2.reshape(batch, seq, d)
```
